```python
import jax, jax.numpy as jnp
from jax import lax
import numpy as np

D_MODEL = 4096
BATCH = 2
SEQ = 4096
DEPTH = 2

D_MIX = D_MODEL
FOX_WIDTH = D_MIX // 2
FOX_HEAD_DIM = 128
FOX_HEADS = FOX_WIDTH // FOX_HEAD_DIM
RWKV_WIDTH = D_MIX - FOX_WIDTH
RWKV_HEAD_DIM = 64
RWKV_HEADS = RWKV_WIDTH // RWKV_HEAD_DIM
DECAY_LORA = max(32, int(round(1.8 * D_MODEL ** 0.5 / 32)) * 32)
AAA_LORA = max(32, int(round(1.8 * D_MODEL ** 0.5 / 32)) * 32)
GATE_LORA = max(32, int(round(0.6 * D_MODEL ** 0.8 / 32)) * 32)
D_FF = 4 * D_MODEL
MEM_TOKENS = 256
MEM_HEADS = 4
MEM_HEAD_DIM = 128
MEM_WIDTH = MEM_HEADS * MEM_HEAD_DIM
BLOCK_Q = 128
NORM_EPS = 1e-6
GN_EPS = 64e-5

FOX_SPLITS = (FOX_WIDTH, FOX_WIDTH, FOX_WIDTH, FOX_WIDTH, FOX_HEADS)
RWKV_SPLITS = (RWKV_WIDTH, DECAY_LORA, RWKV_WIDTH, RWKV_WIDTH, AAA_LORA, GATE_LORA)
FOX_IN = sum(FOX_SPLITS)
RWKV_IN = sum(RWKV_SPLITS)
MIX_IN = FOX_IN + RWKV_IN
FOX_OFFSETS = tuple(int(o) for o in np.cumsum(FOX_SPLITS[:-1]))
RWKV_OFFSETS = tuple(int(o) for o in np.cumsum(RWKV_SPLITS[:-1]))

kernel_name = "hymba_fox_rwkv7_hybrid"


def rms_norm(x, gain, eps=NORM_EPS):
    xf = x.astype(jnp.float32)
    y = xf * lax.rsqrt(jnp.mean(xf * xf, axis=-1, keepdims=True) + eps)
    return (y * gain.astype(jnp.float32)).astype(x.dtype)


def forgetting_attention(q, k, v, log_f):
    B, H, S, Dh = q.shape
    n_blocks = S // BLOCK_Q
    c = jnp.cumsum(log_f, axis=-1)
    q_blocks = q.reshape(B, H, n_blocks, BLOCK_Q, Dh).transpose(2, 0, 1, 3, 4)
    c_blocks = c.reshape(B, H, n_blocks, BLOCK_Q).transpose(2, 0, 1, 3)
    starts = jnp.arange(n_blocks, dtype=jnp.int32) * BLOCK_Q
    key_pos = jnp.arange(S, dtype=jnp.int32)
    scale = Dh ** -0.5

    def one_block(args):
        q_blk, c_blk, start = args
        s = jnp.einsum('bhqd,bhkd->bhqk', q_blk, k, preferred_element_type=jnp.float32) * scale
        s = s + c_blk[..., :, None] - c[..., None, :]
        q_pos = start + jnp.arange(BLOCK_Q, dtype=jnp.int32)
        s = jnp.where(key_pos[None, :] <= q_pos[:, None], s, -jnp.inf)
        p = jax.nn.softmax(s, axis=-1)
        return jnp.einsum('bhqk,bhkd->bhqd', p.astype(v.dtype), v)

    out = lax.map(one_block, (q_blocks, c_blocks, starts))
    return out.transpose(1, 2, 0, 3, 4).reshape(B, H, S, Dh)


def fox_mixer(p, q_gain, k_gain, f_bias):
    B, S, _ = p.shape
    q, k, v, gate, f_logit = jnp.split(p, FOX_OFFSETS, axis=-1)
    shp = (B, S, FOX_HEADS, FOX_HEAD_DIM)
    q = rms_norm(q.reshape(shp), q_gain).transpose(0, 2, 1, 3)
    k = rms_norm(k.reshape(shp), k_gain).transpose(0, 2, 1, 3)
    v = v.reshape(shp).transpose(0, 2, 1, 3)
    log_f = jax.nn.log_sigmoid(f_logit.astype(jnp.float32) + f_bias.astype(jnp.float32)).transpose(0, 2, 1)
    o = forgetting_attention(q, k, v, log_f).transpose(0, 2, 1, 3).reshape(B, S, FOX_WIDTH)
    return o * jax.nn.sigmoid(gate)


def rwkv7_scan(r, w, k, v, a, b):
    B, S, H, N = r.shape

    def step(state, inp):
        r_t, w_t, k_t, v_t, a_t, b_t = inp
        sa = jnp.einsum('bhvk,bhk->bhv', state, a_t)
        state = state * w_t[:, :, None, :] + sa[..., None] * b_t[:, :, None, :] + v_t[..., None] * k_t[:, :, None, :]
        return state, jnp.einsum('bhvk,bhk->bhv', state, r_t)

    xs = tuple(t.transpose(1, 0, 2, 3) for t in (r, w, k, v, a, b))
    init = jnp.zeros((B, H, N, N), jnp.float32)
    _, ys = lax.scan(step, init, xs)
    return ys.transpose(1, 0, 2, 3)


def rwkv7_mixer(p, mu, w0, w2, a0, a2, g2, k_k, k_a, r_k, gn_gain, gn_bias):
    B, S, _ = p.shape
    f32 = jnp.float32
    prev = jnp.pad(p, ((0, 0), (1, 0), (0, 0)))[:, :-1]
    p = (p + (prev - p) * mu).astype(f32)
    r, w_lo, k, v, a_lo, g_lo = jnp.split(p, RWKV_OFFSETS, axis=-1)
    w_log = -jax.nn.softplus(-(w0 + jnp.tanh(w_lo) @ w2)) - 0.5
    decay = jnp.exp(-jnp.exp(w_log))
    a_lr = jax.nn.sigmoid(a0 + a_lo @ a2)
    g = jax.nn.sigmoid(g_lo) @ g2
    heads = lambda t: t.reshape(B, S, RWKV_HEADS, RWKV_HEAD_DIM)
    kk = heads(k * k_k)
    kk = kk * lax.rsqrt(jnp.maximum(jnp.sum(kk * kk, axis=-1, keepdims=True), 1e-24))
    k = k * (1.0 + (a_lr - 1.0) * k_a)
    rh, kh, vh = heads(r), heads(k), heads(v)
    y = rwkv7_scan(rh, heads(decay), kh, vh, -kk, kk * heads(a_lr))
    mean = jnp.mean(y, axis=-1, keepdims=True)
    var = jnp.mean(jnp.square(y - mean), axis=-1, keepdims=True)
    y = ((y - mean) * lax.rsqrt(var + GN_EPS)).reshape(B, S, RWKV_WIDTH) * gn_gain + gn_bias
    bonus = jnp.sum(rh * kh * r_k, axis=-1, keepdims=True) * vh
    y = y + bonus.reshape(B, S, RWKV_WIDTH)
    return (y * g).astype(p.dtype)


def memory_cross_attention(h, mem_n, w_q, w_k, w_v, w_o, q_gain, k_gain):
    B, S, _ = h.shape
    M = mem_n.shape[1]
    q = rms_norm((h @ w_q).reshape(B, S, MEM_HEADS, MEM_HEAD_DIM), q_gain)
    k = rms_norm((mem_n @ w_k).reshape(B, M, MEM_HEADS, MEM_HEAD_DIM), k_gain)
    v = (mem_n @ w_v).reshape(B, M, MEM_HEADS, MEM_HEAD_DIM)
    s = jnp.einsum('bshd,bmhd->bhsm', q, k, preferred_element_type=jnp.float32) * MEM_HEAD_DIM ** -0.5
    p = jax.nn.softmax(s, axis=-1).astype(v.dtype)
    o = jnp.einsum('bhsm,bmhd->bshd', p, v).reshape(B, S, MEM_WIDTH)
    return o @ w_o


def setup_inputs(seed: int = 0) -> dict:
    key = jax.random.key(seed)
    ks = iter(jax.random.split(key, 40))
    f32 = jnp.float32
    L = DEPTH
    res = (3 * DEPTH) ** -0.5

    def nrm(shape, scale):
        return jax.random.normal(next(ks), shape, f32) * scale

    def gain(shape):
        return 1.0 + nrm(shape, 0.02)

    return {
        "x": nrm((BATCH, SEQ, D_MODEL), 1.0),
        "mem": nrm((BATCH, MEM_TOKENS, D_MODEL), 1.0),
        "norm_mix": gain((L, D_MODEL)),
        "w_in": nrm((L, D_MODEL, MIX_IN), D_MODEL ** -0.5),
        "fox_q_gain": gain((L, FOX_HEAD_DIM)),
        "fox_k_gain": gain((L, FOX_HEAD_DIM)),
        "fox_f_bias": 2.0 + nrm((L, FOX_HEADS), 0.5),
        "rwkv_mu": jax.random.uniform(next(ks), (L, RWKV_IN), f32),
        "rwkv_w0": nrm((L, RWKV_WIDTH), 0.5),
        "rwkv_w2": nrm((L, DECAY_LORA, RWKV_WIDTH), 0.1 * DECAY_LORA ** -0.5),
        "rwkv_a0": nrm((L, RWKV_WIDTH), 0.1),
        "rwkv_a2": nrm((L, AAA_LORA, RWKV_WIDTH), 0.1 * AAA_LORA ** -0.5),
        "rwkv_g2": nrm((L, GATE_LORA, RWKV_WIDTH), GATE_LORA ** -0.5),
        "rwkv_k_k": 0.85 + nrm((L, RWKV_WIDTH), 0.02),
        "rwkv_k_a": gain((L, RWKV_WIDTH)),
        "rwkv_r_k": nrm((L, RWKV_HEADS, RWKV_HEAD_DIM), 0.1),
        "rwkv_gn_gain": gain((L, RWKV_WIDTH)),
        "rwkv_gn_bias": nrm((L, RWKV_WIDTH), 0.01),
        "w_out": nrm((L, D_MIX, D_MODEL), res * D_MIX ** -0.5),
        "norm_mem_q": gain((L, D_MODEL)),
        "norm_mem_kv": gain((L, D_MODEL)),
        "mem_w_q": nrm((L, D_MODEL, MEM_WIDTH), D_MODEL ** -0.5),
        "mem_w_k": nrm((L, D_MODEL, MEM_WIDTH), D_MODEL ** -0.5),
        "mem_w_v": nrm((L, D_MODEL, MEM_WIDTH), D_MODEL ** -0.5),
        "mem_q_gain": gain((L, MEM_HEAD_DIM)),
        "mem_k_gain": gain((L, MEM_HEAD_DIM)),
        "mem_w_o": nrm((L, MEM_WIDTH, D_MODEL), res * MEM_WIDTH ** -0.5),
        "norm_mlp": gain((L, D_MODEL)),
        "w_up": nrm((L, D_MODEL, D_FF), D_MODEL ** -0.5),
        "w_down": nrm((L, D_FF, D_MODEL), res * D_FF ** -0.5),
    }


def reference(x, mem, norm_mix, w_in, fox_q_gain, fox_k_gain, fox_f_bias, rwkv_mu, rwkv_w0, rwkv_w2,
              rwkv_a0, rwkv_a2, rwkv_g2, rwkv_k_k, rwkv_k_a, rwkv_r_k, rwkv_gn_gain, rwkv_gn_bias, w_out,
              norm_mem_q, norm_mem_kv, mem_w_q, mem_w_k, mem_w_v, mem_q_gain, mem_k_gain, mem_w_o,
              norm_mlp, w_up, w_down):
    for l in range(DEPTH):
        h = rms_norm(x, norm_mix[l])
        proj = h @ w_in[l]
        y_fox = fox_mixer(proj[..., :FOX_IN], fox_q_gain[l], fox_k_gain[l], fox_f_bias[l])
        y_rwkv = rwkv7_mixer(proj[..., FOX_IN:], rwkv_mu[l], rwkv_w0[l], rwkv_w2[l], rwkv_a0[l], rwkv_a2[l],
                             rwkv_g2[l], rwkv_k_k[l], rwkv_k_a[l], rwkv_r_k[l], rwkv_gn_gain[l], rwkv_gn_bias[l])
        x = x + jnp.concatenate([y_fox, y_rwkv.astype(y_fox.dtype)], axis=-1) @ w_out[l]
        h = rms_norm(x, norm_mem_q[l])
        m = rms_norm(mem, norm_mem_kv[l])
        x = x + memory_cross_attention(h, m, mem_w_q[l], mem_w_k[l], mem_w_v[l], mem_w_o[l],
                                       mem_q_gain[l], mem_k_gain[l])
        h = rms_norm(x, norm_mlp[l])
        x = x + jnp.square(jax.nn.relu(h @ w_up[l])) @ w_down[l]
    return x
```

```python
import functools

import jax
import jax.numpy as jnp
from jax import lax
from jax.experimental import pallas as pl
from jax.experimental.pallas import tpu as pltpu

D_MODEL = 4096
FOX_WIDTH = 2048
FOX_HEAD_DIM = 128
FOX_HEADS = 16
RWKV_WIDTH = 2048
RWKV_HEAD_DIM = 64
RWKV_HEADS = 32
DECAY_LORA = 128
AAA_LORA = 128
GATE_LORA = 480
D_FF = 4 * D_MODEL
MEM_HEADS = 4
MEM_HEAD_DIM = 128
MEM_WIDTH = MEM_HEADS * MEM_HEAD_DIM
NORM_EPS = 1e-6
GN_EPS = 64e-5
FOX_QKVG = 4 * FOX_WIDTH
FOX_IN = FOX_QKVG + FOX_HEADS
RWKV_IN = 3 * RWKV_WIDTH + DECAY_LORA + AAA_LORA + GATE_LORA
R_OFF, WLO_OFF, K_OFF, V_OFF, ALO_OFF, GLO_OFF = 0, 2048, 2176, 4224, 6272, 6400

LANES = 128
SUBLANES = 8
VMEM_BYTES_V7X = 64 * 1024 * 1024
RWKV_PAD = 6912
GATE_PAD = RWKV_PAD - GLO_OFF

F32 = jnp.float32
BF16 = jnp.bfloat16
HIGHEST = lax.Precision.HIGHEST


def _vmem_limit(block_bytes, scratch_bytes=0, temp_bytes=0):
    need = 2 * block_bytes + scratch_bytes + temp_bytes + (4 << 20)
    return int(min(need, VMEM_BYTES_V7X - (8 << 20)))


def _nbytes(shape, dtype):
    n = 1
    for s in shape:
        n *= s
    return n * jnp.dtype(dtype).itemsize


def _rmsnorm_kernel(x_ref, g_ref, o_ref):
    x = x_ref[...]
    ms = jnp.mean(x * x, axis=-1, keepdims=True)
    o_ref[...] = (x * lax.rsqrt(ms + NORM_EPS) * g_ref[...]).astype(o_ref.dtype)


def rmsnorm_bf16(x, gain, tm=256):
    m, d = x.shape
    tm = min(tm, m)
    return pl.pallas_call(
        _rmsnorm_kernel,
        grid=(m // tm,),
        in_specs=[pl.BlockSpec((tm, d), lambda i: (i, 0)),
                  pl.BlockSpec((1, d), lambda i: (0, 0))],
        out_specs=pl.BlockSpec((tm, d), lambda i: (i, 0)),
        out_shape=jax.ShapeDtypeStruct((m, d), BF16),
        compiler_params=pltpu.CompilerParams(
            dimension_semantics=("parallel",),
            vmem_limit_bytes=_vmem_limit(_nbytes((tm, d), F32) + _nbytes((tm, d), BF16),
                                         temp_bytes=2 * _nbytes((tm, d), F32))),
        name="rmsnorm",
    )(x, gain.reshape(1, d))


def _matmul_kernel(*refs, epilogue, nk):
    if epilogue == "residual":
        a_ref, w_ref, r_ref, o_ref, acc_ref = refs
    else:
        a_ref, w_ref, o_ref, acc_ref = refs
    k = pl.program_id(2)

    @pl.when(k == 0)
    def _():
        acc_ref[...] = jnp.zeros_like(acc_ref)

    acc_ref[...] += jnp.dot(a_ref[...], w_ref[...], preferred_element_type=F32)

    @pl.when(k == nk - 1)
    def _():
        acc = acc_ref[...]
        if epilogue == "relu2":
            r = jnp.maximum(acc, 0.0)
            o_ref[...] = (r * r).astype(o_ref.dtype)
        elif epilogue == "residual":
            o_ref[...] = r_ref[...] + acc
        else:
            o_ref[...] = acc.astype(o_ref.dtype)


def matmul(a, w, *, epilogue="none", residual=None, out_dtype=F32, tm=1024, tn=1024, tk=1024):
    m, kdim = a.shape
    _, n = w.shape
    tm, tn, tk = min(tm, m), min(tn, n), min(tk, kdim)
    assert m % tm == 0 and n % tn == 0 and kdim % tk == 0, (a.shape, w.shape, tm, tn, tk)
    nk = kdim // tk
    in_specs = [pl.BlockSpec((tm, tk), lambda i, j, k: (i, k)),
                pl.BlockSpec((tk, tn), lambda i, j, k: (k, j))]
    args = [a, w]
    blk = _nbytes((tm, tk), BF16) + _nbytes((tk, tn), BF16) + _nbytes((tm, tn), out_dtype)
    if epilogue == "residual":
        in_specs.append(pl.BlockSpec((tm, tn), lambda i, j, k: (i, j)))
        args.append(residual)
        blk += _nbytes((tm, tn), F32)
    return pl.pallas_call(
        functools.partial(_matmul_kernel, epilogue=epilogue, nk=nk),
        grid=(m // tm, n // tn, nk),
        in_specs=in_specs,
        out_specs=pl.BlockSpec((tm, tn), lambda i, j, k: (i, j)),
        out_shape=jax.ShapeDtypeStruct((m, n), out_dtype),
        scratch_shapes=[pltpu.VMEM((tm, tn), F32)],
        compiler_params=pltpu.CompilerParams(
            dimension_semantics=("parallel", "parallel", "arbitrary"),
            vmem_limit_bytes=_vmem_limit(blk, scratch_bytes=_nbytes((tm, tn), F32),
                                         temp_bytes=2 * _nbytes((tm, tn), F32))),
        name="matmul_" + epilogue,
    )(*args)


def _fox_prep_kernel(p_ref, h_ref, wf_ref, fb_ref, qg_ref, kg_ref, tri_ref,
                     q_o, k_o, v_o, c_o, carry_ref, *, tq):
    @pl.when(pl.program_id(1) == 0)
    def _():
        carry_ref[...] = jnp.zeros_like(carry_ref)

    scale = FOX_HEAD_DIM ** -0.5
    for h in range(FOX_HEADS):
        lo = h * FOX_HEAD_DIM
        q = p_ref[:, lo:lo + FOX_HEAD_DIM]
        ms = jnp.mean(q * q, axis=-1, keepdims=True)
        q_o[:, lo:lo + FOX_HEAD_DIM] = (q * lax.rsqrt(ms + NORM_EPS) * qg_ref[...] * scale).astype(BF16)
        k = p_ref[:, FOX_WIDTH + lo:FOX_WIDTH + lo + FOX_HEAD_DIM]
        ms = jnp.mean(k * k, axis=-1, keepdims=True)
        k_o[:, lo:lo + FOX_HEAD_DIM] = (k * lax.rsqrt(ms + NORM_EPS) * kg_ref[...]).astype(BF16)
    v_o[...] = p_ref[:, 2 * FOX_WIDTH:3 * FOX_WIDTH].astype(BF16)

    f_logit = lax.dot_general(wf_ref[...], h_ref[...], (((1,), (1,)), ((), ())),
                              preferred_element_type=F32)
    log_f = jax.nn.log_sigmoid(f_logit + fb_ref[...])
    c = jnp.dot(log_f, tri_ref[...], precision=HIGHEST, preferred_element_type=F32) + carry_ref[...]
    c_o[...] = c
    carry_ref[...] = jnp.broadcast_to(c[:, tq - 1:tq], carry_ref.shape)


def fox_prep(pf3, h3, wf_t, f_bias, q_gain, k_gain, tq=512):
    b, s, _ = pf3.shape
    tq = min(tq, s)
    tri = (lax.broadcasted_iota(jnp.int32, (tq, tq), 0) <= lax.broadcasted_iota(jnp.int32, (tq, tq), 1)).astype(F32)
    qkv_w = 3 * FOX_WIDTH
    act = jax.ShapeDtypeStruct((b, s, FOX_WIDTH), BF16)
    blk = (_nbytes((tq, qkv_w), F32) + _nbytes((tq, D_MODEL), BF16) + _nbytes((FOX_HEADS, D_MODEL), BF16)
           + _nbytes((tq, tq), F32) + 3 * _nbytes((tq, FOX_WIDTH), BF16) + _nbytes((FOX_HEADS, tq), F32))
    return pl.pallas_call(
        functools.partial(_fox_prep_kernel, tq=tq),
        grid=(b, s // tq),
        in_specs=[pl.BlockSpec((None, tq, qkv_w), lambda bi, i: (bi, i, 0)),
                  pl.BlockSpec((None, tq, D_MODEL), lambda bi, i: (bi, i, 0)),
                  pl.BlockSpec((FOX_HEADS, D_MODEL), lambda bi, i: (0, 0)),
                  pl.BlockSpec((FOX_HEADS, 1), lambda bi, i: (0, 0)),
                  pl.BlockSpec((1, FOX_HEAD_DIM), lambda bi, i: (0, 0)),
                  pl.BlockSpec((1, FOX_HEAD_DIM), lambda bi, i: (0, 0)),
                  pl.BlockSpec((tq, tq), lambda bi, i: (0, 0))],
        out_specs=[pl.BlockSpec((None, tq, FOX_WIDTH), lambda bi, i: (bi, i, 0)),
                   pl.BlockSpec((None, tq, FOX_WIDTH), lambda bi, i: (bi, i, 0)),
                   pl.BlockSpec((None, tq, FOX_WIDTH), lambda bi, i: (bi, i, 0)),
                   pl.BlockSpec((None, FOX_HEADS, tq), lambda bi, i: (bi, 0, i))],
        out_shape=[act, act, act, jax.ShapeDtypeStruct((b, FOX_HEADS, s), F32)],
        scratch_shapes=[pltpu.VMEM((FOX_HEADS, tq), F32)],
        compiler_params=pltpu.CompilerParams(
            dimension_semantics=("parallel", "arbitrary"),
            vmem_limit_bytes=_vmem_limit(blk, temp_bytes=_nbytes((tq, qkv_w), F32))),
        name="fox_prep",
    )(pf3, h3, wf_t, f_bias.reshape(FOX_HEADS, 1), q_gain.reshape(1, -1), k_gain.reshape(1, -1), tri)


def _fox_attn_kernel(q_ref, k_ref, v_ref, c_ref, gate_ref, o_ref, m_ref, l_ref, acc_ref, *, t):
    qi = pl.program_id(2)
    m_ref[...] = jnp.full_like(m_ref, -jnp.inf)
    l_ref[...] = jnp.zeros_like(l_ref)
    acc_ref[...] = jnp.zeros_like(acc_ref)
    q = q_ref[...]
    c_base = c_ref[pl.ds(qi, 1), :][:, 0:1]

    def block(ki, masked):
        start = pl.multiple_of(ki * t, t)
        k = k_ref[pl.ds(start, t), :]
        v = v_ref[pl.ds(start, t), :]
        s = lax.dot_general(q, k, (((1,), (1,)), ((), ())), preferred_element_type=F32)
        s = s + (c_base - c_ref[pl.ds(ki, 1), :])
        if masked:
            row = lax.broadcasted_iota(jnp.int32, (t, t), 0)
            col = lax.broadcasted_iota(jnp.int32, (t, t), 1)
            s = jnp.where(col <= row, s, -jnp.inf)
        m_prev = m_ref[...]
        m_new = jnp.maximum(m_prev, jnp.max(s, axis=1, keepdims=True))
        alpha = jnp.exp(m_prev - m_new)
        p = jnp.exp(s - jnp.tile(m_new, (1, t // LANES)))
        l_ref[...] = alpha * l_ref[...] + jnp.sum(p, axis=1, keepdims=True)
        acc_ref[...] = alpha * acc_ref[...] + jnp.dot(p.astype(BF16), v, preferred_element_type=F32)
        m_ref[...] = m_new

    def body(ki, carry):
        block(ki, False)
        return carry

    lax.fori_loop(0, qi, body, 0)
    block(qi, True)
    o = acc_ref[...] / l_ref[...]
    o_ref[...] = (o * jax.nn.sigmoid(gate_ref[...])).astype(o_ref.dtype)


def fox_attention(qn, kn, vb, c, pf3, t=512):
    b, s, _ = qn.shape
    t = min(t, s)
    nt = s // t
    c4 = c.reshape(b, FOX_HEADS, nt, t)
    gate_blk0 = 3 * FOX_WIDTH // FOX_HEAD_DIM
    blk = (2 * _nbytes((t, LANES), BF16) + 2 * _nbytes((s, LANES), BF16) + _nbytes((nt, t), F32)
           + _nbytes((t, LANES), F32))
    return pl.pallas_call(
        functools.partial(_fox_attn_kernel, t=t),
        grid=(b, FOX_HEADS, nt),
        in_specs=[pl.BlockSpec((None, t, FOX_HEAD_DIM), lambda bi, h, qi: (bi, qi, h)),
                  pl.BlockSpec((None, s, FOX_HEAD_DIM), lambda bi, h, qi: (bi, 0, h)),
                  pl.BlockSpec((None, s, FOX_HEAD_DIM), lambda bi, h, qi: (bi, 0, h)),
                  pl.BlockSpec((None, None, nt, t), lambda bi, h, qi: (bi, h, 0, 0)),
                  pl.BlockSpec((None, t, FOX_HEAD_DIM), lambda bi, h, qi: (bi, qi, gate_blk0 + h))],
        out_specs=pl.BlockSpec((None, t, FOX_HEAD_DIM), lambda bi, h, qi: (bi, qi, h)),
        out_shape=jax.ShapeDtypeStruct((b, s, FOX_WIDTH), BF16),
        scratch_shapes=[pltpu.VMEM((t, LANES), F32), pltpu.VMEM((t, LANES), F32),
                        pltpu.VMEM((t, FOX_HEAD_DIM), F32)],
        compiler_params=pltpu.CompilerParams(
            dimension_semantics=("parallel", "parallel", "arbitrary"),
            vmem_limit_bytes=_vmem_limit(blk, scratch_bytes=3 * _nbytes((t, LANES), F32),
                                         temp_bytes=6 * _nbytes((t, t), F32))),
        name="fox_attention",
    )(qn, kn, vb, c4, pf3)


def _head_sum(x, p_ref, pt_ref):
    s = jnp.dot(x, p_ref[...], precision=HIGHEST, preferred_element_type=F32)
    return jnp.dot(s, pt_ref[...], precision=HIGHEST, preferred_element_type=F32)


def _rwkv_prep_kernel(p_ref, pprev_ref, mu_ref, w0_ref, w2_ref, a0_ref, a2_ref, g2_ref, kk_ref, ka_ref,
                      rk_ref, hp_ref, hpt_ref, r_o, w_o, k_o, v_o, a_o, b_o, g_o, bonus_o,
                      *, tiles_per_seq):
    first = (pl.program_id(0) % tiles_per_seq) == 0

    def shifted(lo, hi):
        p = p_ref[:, lo:hi]
        prev_row = jnp.where(first, 0.0, pprev_ref[SUBLANES - 1:SUBLANES, lo:hi])
        row = lax.broadcasted_iota(jnp.int32, p.shape, 0)
        prev = jnp.where(row == 0, prev_row, pltpu.roll(p, 1, axis=0))
        return p + (prev - p) * mu_ref[:, lo:hi]

    w_lo = shifted(WLO_OFF, WLO_OFF + DECAY_LORA)
    z = w0_ref[...] + jnp.dot(jnp.tanh(w_lo), w2_ref[...], precision=HIGHEST, preferred_element_type=F32)
    softplus_neg = jnp.maximum(-z, 0.0) + jnp.log1p(jnp.exp(-jnp.abs(z)))
    w_o[...] = jnp.exp(-jnp.exp(-softplus_neg - 0.5))

    a_lo = shifted(ALO_OFF, ALO_OFF + AAA_LORA)
    a_lr = jax.nn.sigmoid(a0_ref[...] + jnp.dot(a_lo, a2_ref[...], precision=HIGHEST,
                                                preferred_element_type=F32))

    g_lo = shifted(GLO_OFF, GLO_OFF + GATE_PAD)
    g_o[...] = jnp.dot(jax.nn.sigmoid(g_lo).astype(BF16), g2_ref[...], preferred_element_type=F32)

    k = shifted(K_OFF, K_OFF + RWKV_WIDTH)
    kk = k * kk_ref[...]
    kk = kk * lax.rsqrt(jnp.maximum(_head_sum(kk * kk, hp_ref, hpt_ref), 1e-24))
    a_o[...] = -kk
    b_o[...] = kk * a_lr
    k = k * (1.0 + (a_lr - 1.0) * ka_ref[...])
    k_o[...] = k

    r = shifted(R_OFF, R_OFF + RWKV_WIDTH)
    r_o[...] = r
    v = shifted(V_OFF, V_OFF + RWKV_WIDTH)
    v_o[...] = v
    bonus_o[...] = _head_sum(r * k * rk_ref[...], hp_ref, hpt_ref) * v


def _head_indicator():
    col = lax.broadcasted_iota(jnp.int32, (RWKV_WIDTH, LANES), 0) // RWKV_HEAD_DIM
    head = lax.broadcasted_iota(jnp.int32, (RWKV_WIDTH, LANES), 1)
    hp = (col == head).astype(F32)
    return hp, hp.T


def rwkv_prep(pr, seq, mu, w0, w2, a0, a2, g2p, k_k, k_a, r_k, tq=128):
    t_total, width = pr.shape
    tq = min(tq, seq)
    hp, hpt = _head_indicator()
    row = lambda x: x.reshape(1, -1)
    full = lambda shape: pl.BlockSpec(shape, lambda i: (0, 0))
    out = jax.ShapeDtypeStruct((t_total, RWKV_WIDTH), F32)
    out_spec = pl.BlockSpec((tq, RWKV_WIDTH), lambda i: (i, 0))
    sub_per_tile = tq // SUBLANES
    blk = (_nbytes((tq, width), F32) + _nbytes((SUBLANES, width), F32) + 8 * _nbytes((tq, RWKV_WIDTH), F32)
           + _nbytes((DECAY_LORA + AAA_LORA, RWKV_WIDTH), F32) + _nbytes((GATE_PAD, RWKV_WIDTH), BF16)
           + 2 * _nbytes((RWKV_WIDTH, LANES), F32))
    return pl.pallas_call(
        functools.partial(_rwkv_prep_kernel, tiles_per_seq=seq // tq),
        grid=(t_total // tq,),
        in_specs=[pl.BlockSpec((tq, width), lambda i: (i, 0)),
                  pl.BlockSpec((SUBLANES, width), lambda i: (jnp.maximum(i * sub_per_tile - 1, 0), 0)),
                  full((1, width)), full((1, RWKV_WIDTH)), full((DECAY_LORA, RWKV_WIDTH)),
                  full((1, RWKV_WIDTH)), full((AAA_LORA, RWKV_WIDTH)), full((GATE_PAD, RWKV_WIDTH)),
                  full((1, RWKV_WIDTH)), full((1, RWKV_WIDTH)), full((1, RWKV_WIDTH)),
                  full((RWKV_WIDTH, LANES)), full((LANES, RWKV_WIDTH))],
        out_specs=[out_spec] * 8,
        out_shape=[out] * 8,
        compiler_params=pltpu.CompilerParams(
            dimension_semantics=("parallel",),
            vmem_limit_bytes=_vmem_limit(blk, temp_bytes=12 * _nbytes((tq, RWKV_WIDTH), F32))),
        name="rwkv_prep",
    )(pr, pr, row(mu), row(w0), w2, row(a0), a2, g2p, row(k_k), row(k_a), row(r_k), hp, hpt)


V_ROWS = RWKV_HEAD_DIM // 2
V_GROUPS = V_ROWS // SUBLANES


def _rwkv_scan_kernel(r_ref, w_ref, k_ref, v_ref, a_ref, b_ref, y_ref, state_ref, *, tc):
    @pl.when(pl.program_id(0) == 0)
    def _():
        state_ref[...] = jnp.zeros_like(state_ref)

    def bcast(ref, t, k):
        return jnp.broadcast_to(ref[t, k:k + 1, :], (SUBLANES, LANES))

    def group(g):
        return pl.ds(g * SUBLANES, SUBLANES)

    def step(t, carry):
        sa = [jnp.zeros((SUBLANES, LANES), F32) for _ in range(2 * V_GROUPS)]
        for k in range(RWKV_HEAD_DIM):
            ab = bcast(a_ref, t, k)
            for g in range(V_GROUPS):
                sa[2 * g + k % 2] += state_ref[k, group(g), :] * ab
        sa = [sa[2 * g] + sa[2 * g + 1] for g in range(V_GROUPS)]
        vt = [v_ref[t, group(g), :] for g in range(V_GROUPS)]
        y = [jnp.zeros((SUBLANES, LANES), F32) for _ in range(2 * V_GROUPS)]
        for k in range(RWKV_HEAD_DIM):
            wb, bb, kb, rb = bcast(w_ref, t, k), bcast(b_ref, t, k), bcast(k_ref, t, k), bcast(r_ref, t, k)
            for g in range(V_GROUPS):
                s = state_ref[k, group(g), :] * wb + sa[g] * bb + vt[g] * kb
                state_ref[k, group(g), :] = s
                y[2 * g + k % 2] += s * rb
        for g in range(V_GROUPS):
            y_ref[t, group(g), :] = y[2 * g] + y[2 * g + 1]
        return carry

    lax.fori_loop(0, tc, step, 0)


def rwkv_scan(r_t, w_t, k_t, v_t, a_t, b_t, tc=32):
    s = r_t.shape[0]
    tc = min(tc, s)
    kspec = pl.BlockSpec((tc, RWKV_HEAD_DIM, LANES), lambda i: (i, 0, 0))
    vspec = pl.BlockSpec((tc, V_ROWS, LANES), lambda i: (i, 0, 0))
    blk = 5 * _nbytes((tc, RWKV_HEAD_DIM, LANES), F32) + 2 * _nbytes((tc, V_ROWS, LANES), F32)
    return pl.pallas_call(
        functools.partial(_rwkv_scan_kernel, tc=tc),
        grid=(s // tc,),
        in_specs=[kspec, kspec, kspec, vspec, kspec, kspec],
        out_specs=vspec,
        out_shape=jax.ShapeDtypeStruct((s, V_ROWS, LANES), F32),
        scratch_shapes=[pltpu.VMEM((RWKV_HEAD_DIM, V_ROWS, LANES), F32)],
        compiler_params=pltpu.CompilerParams(
            dimension_semantics=("arbitrary",),
            vmem_limit_bytes=_vmem_limit(blk, scratch_bytes=_nbytes((RWKV_HEAD_DIM, V_ROWS, LANES), F32))),
        name="rwkv_scan",
    )(r_t, w_t, k_t, v_t, a_t, b_t)


def _rwkv_finalize_kernel(y_ref, bonus_ref, g_ref, gain_ref, bias_ref, hp_ref, hpt_ref, o_ref):
    y = y_ref[...]
    inv_n = 1.0 / RWKV_HEAD_DIM
    mean = _head_sum(y, hp_ref, hpt_ref) * inv_n
    yc = y - mean
    var = _head_sum(yc * yc, hp_ref, hpt_ref) * inv_n
    yn = yc * lax.rsqrt(var + GN_EPS) * gain_ref[...] + bias_ref[...]
    o_ref[...] = ((yn + bonus_ref[...]) * g_ref[...]).astype(o_ref.dtype)


def rwkv_finalize(y, bonus, g, gn_gain, gn_bias, tq=256):
    t_total = y.shape[0]
    tq = min(tq, t_total)
    hp, hpt = _head_indicator()
    tile = pl.BlockSpec((tq, RWKV_WIDTH), lambda i: (i, 0))
    full = lambda shape: pl.BlockSpec(shape, lambda i: (0, 0))
    blk = 3 * _nbytes((tq, RWKV_WIDTH), F32) + _nbytes((tq, RWKV_WIDTH), BF16) + 2 * _nbytes((RWKV_WIDTH, LANES), F32)
    return pl.pallas_call(
        _rwkv_finalize_kernel,
        grid=(t_total // tq,),
        in_specs=[tile, tile, tile, full((1, RWKV_WIDTH)), full((1, RWKV_WIDTH)),
                  full((RWKV_WIDTH, LANES)), full((LANES, RWKV_WIDTH))],
        out_specs=tile,
        out_shape=jax.ShapeDtypeStruct((t_total, RWKV_WIDTH), BF16),
        compiler_params=pltpu.CompilerParams(
            dimension_semantics=("parallel",),
            vmem_limit_bytes=_vmem_limit(blk, temp_bytes=8 * _nbytes((tq, RWKV_WIDTH), F32))),
        name="rwkv_finalize",
    )(y, bonus, g, gn_gain.reshape(1, -1), gn_bias.reshape(1, -1), hp, hpt)


def _to_head_lanes(x, b, s):
    x = x.reshape(b, s, RWKV_HEADS, RWKV_HEAD_DIM).transpose(1, 3, 0, 2).reshape(s, RWKV_HEAD_DIM, b * RWKV_HEADS)
    return jnp.concatenate([x, x], axis=-1)


def _values_to_head_lanes(v, b, s):
    v = v.reshape(b, s, RWKV_HEADS, 2, V_ROWS).transpose(1, 4, 3, 0, 2)
    return v.reshape(s, V_ROWS, 2 * b * RWKV_HEADS)


def _values_from_head_lanes(y, b, s):
    y = y.reshape(s, V_ROWS, 2, b, RWKV_HEADS).transpose(3, 0, 4, 2, 1)
    return y.reshape(b * s, RWKV_WIDTH)


def _mem_attn_kernel(q_ref, k_ref, v_ref, qg_ref, kg_ref, o_ref):
    scale = MEM_HEAD_DIM ** -0.5
    for h in range(MEM_HEADS):
        cols = slice(h * MEM_HEAD_DIM, (h + 1) * MEM_HEAD_DIM)
        q = q_ref[:, cols]
        q = q * lax.rsqrt(jnp.mean(q * q, axis=-1, keepdims=True) + NORM_EPS) * qg_ref[...]
        k = k_ref[:, cols]
        k = k * lax.rsqrt(jnp.mean(k * k, axis=-1, keepdims=True) + NORM_EPS) * kg_ref[...]
        s = lax.dot_general(q.astype(BF16), k.astype(BF16), (((1,), (1,)), ((), ())),
                            preferred_element_type=F32) * scale
        p = jnp.exp(s - jnp.max(s, axis=-1, keepdims=True))
        p = p / jnp.sum(p, axis=-1, keepdims=True)
        o = jnp.dot(p.astype(BF16), v_ref[:, cols].astype(BF16), preferred_element_type=F32)
        o_ref[:, cols] = o.astype(o_ref.dtype)


def mem_attention(q3, k3, v3, q_gain, k_gain, tq=512):
    b, s, _ = q3.shape
    m = k3.shape[1]
    tq = min(tq, s)
    blk = _nbytes((tq, MEM_WIDTH), F32) + 2 * _nbytes((m, MEM_WIDTH), F32) + _nbytes((tq, MEM_WIDTH), BF16)
    return pl.pallas_call(
        _mem_attn_kernel,
        grid=(b, s // tq),
        in_specs=[pl.BlockSpec((None, tq, MEM_WIDTH), lambda bi, i: (bi, i, 0)),
                  pl.BlockSpec((None, m, MEM_WIDTH), lambda bi, i: (bi, 0, 0)),
                  pl.BlockSpec((None, m, MEM_WIDTH), lambda bi, i: (bi, 0, 0)),
                  pl.BlockSpec((1, MEM_HEAD_DIM), lambda bi, i: (0, 0)),
                  pl.BlockSpec((1, MEM_HEAD_DIM), lambda bi, i: (0, 0))],
        out_specs=pl.BlockSpec((None, tq, MEM_WIDTH), lambda bi, i: (bi, i, 0)),
        out_shape=jax.ShapeDtypeStruct((b, s, MEM_WIDTH), BF16),
        compiler_params=pltpu.CompilerParams(
            dimension_semantics=("parallel", "parallel"),
            vmem_limit_bytes=_vmem_limit(blk, temp_bytes=8 * _nbytes((tq, m), F32))),
        name="mem_attention",
    )(q3, k3, v3, q_gain.reshape(1, -1), k_gain.reshape(1, -1))


def _layer(x, mem2, b, s, p):
    t_total = b * s
    h = rmsnorm_bf16(x, p["norm_mix"])
    w_in = p["w_in"]
    w_fox = w_in[:, :FOX_QKVG].astype(BF16)
    wf_t = w_in[:, FOX_QKVG:FOX_IN].T.astype(BF16)
    w_rwkv = jnp.pad(w_in[:, FOX_IN:], ((0, 0), (0, RWKV_PAD - RWKV_IN))).astype(BF16)
    pf = matmul(h, w_fox)
    pr = matmul(h, w_rwkv, tn=768)

    pf3 = pf.reshape(b, s, FOX_QKVG)
    qn, kn, vb, c = fox_prep(pf3, h.reshape(b, s, D_MODEL), wf_t, p["fox_f_bias"],
                             p["fox_q_gain"], p["fox_k_gain"])
    y_fox = fox_attention(qn, kn, vb, c, pf3).reshape(t_total, FOX_WIDTH)

    mu = jnp.pad(p["rwkv_mu"], (0, RWKV_PAD - RWKV_IN))
    g2p = jnp.pad(p["rwkv_g2"], ((0, GATE_PAD - GATE_LORA), (0, 0))).astype(BF16)
    r, w, k, v, a, bb, g, bonus = rwkv_prep(pr, s, mu, p["rwkv_w0"], p["rwkv_w2"], p["rwkv_a0"], p["rwkv_a2"],
                                            g2p, p["rwkv_k_k"], p["rwkv_k_a"], p["rwkv_r_k"].reshape(-1))
    y_t = rwkv_scan(_to_head_lanes(r, b, s), _to_head_lanes(w, b, s), _to_head_lanes(k, b, s),
                    _values_to_head_lanes(v, b, s), _to_head_lanes(a, b, s), _to_head_lanes(bb, b, s))
    y_rwkv = rwkv_finalize(_values_from_head_lanes(y_t, b, s), bonus, g, p["rwkv_gn_gain"], p["rwkv_gn_bias"])

    y = jnp.concatenate([y_fox, y_rwkv], axis=-1)
    x = matmul(y, p["w_out"].astype(BF16), epilogue="residual", residual=x)

    h = rmsnorm_bf16(x, p["norm_mem_q"])
    m = rmsnorm_bf16(mem2, p["norm_mem_kv"])
    q = matmul(h, p["mem_w_q"].astype(BF16), tn=MEM_WIDTH)
    km = matmul(m, p["mem_w_k"].astype(BF16), tn=MEM_WIDTH)
    vm = matmul(m, p["mem_w_v"].astype(BF16), tn=MEM_WIDTH)
    n_mem = mem2.shape[0] // b
    o = mem_attention(q.reshape(b, s, MEM_WIDTH), km.reshape(b, n_mem, MEM_WIDTH),
                      vm.reshape(b, n_mem, MEM_WIDTH), p["mem_q_gain"], p["mem_k_gain"])
    x = matmul(o.reshape(t_total, MEM_WIDTH), p["mem_w_o"].astype(BF16), epilogue="residual", residual=x)

    h = rmsnorm_bf16(x, p["norm_mlp"])
    u = matmul(h, p["w_up"].astype(BF16), epilogue="relu2", out_dtype=BF16)
    x = matmul(u, p["w_down"].astype(BF16), epilogue="residual", residual=x)
    return x


_PARAM_NAMES = ("norm_mix", "w_in", "fox_q_gain", "fox_k_gain", "fox_f_bias", "rwkv_mu", "rwkv_w0", "rwkv_w2",
                "rwkv_a0", "rwkv_a2", "rwkv_g2", "rwkv_k_k", "rwkv_k_a", "rwkv_r_k", "rwkv_gn_gain",
                "rwkv_gn_bias", "w_out", "norm_mem_q", "norm_mem_kv", "mem_w_q", "mem_w_k", "mem_w_v",
                "mem_q_gain", "mem_k_gain", "mem_w_o", "norm_mlp", "w_up", "w_down")


def kernel(x, mem, norm_mix, w_in, fox_q_gain, fox_k_gain, fox_f_bias, rwkv_mu, rwkv_w0, rwkv_w2, rwkv_a0, rwkv_a2, rwkv_g2, rwkv_k_k, rwkv_k_a, rwkv_r_k, rwkv_gn_gain, rwkv_gn_bias, w_out, norm_mem_q, norm_mem_kv, mem_w_q, mem_w_k, mem_w_v, mem_q_gain, mem_k_gain, mem_w_o, norm_mlp, w_up, w_down):
    params = dict(zip(_PARAM_NAMES, (norm_mix, w_in, fox_q_gain, fox_k_gain, fox_f_bias, rwkv_mu, rwkv_w0,
                                     rwkv_w2, rwkv_a0, rwkv_a2, rwkv_g2, rwkv_k_k, rwkv_k_a, rwkv_r_k,
                                     rwkv_gn_gain, rwkv_gn_bias, w_out, norm_mem_q, norm_mem_kv, mem_w_q,
                                     mem_w_k, mem_w_v, mem_q_gain, mem_k_gain, mem_w_o, norm_mlp, w_up, w_down)))
    b, s, d = x.shape
    assert d == D_MODEL and (b * RWKV_HEADS * 2) == LANES, x.shape
    depth = w_in.shape[0]
    x2 = x.reshape(b * s, d)
    mem2 = mem.reshape(-1, d)
    for layer in range(depth):
        x2 = _layer(x2, mem2, b, s, {name: value[layer] for name, value in params.items()})
    return x2.reshape(b, s, d)
```

```python
import functools

import jax
import jax.numpy as jnp
from jax import lax
from jax.experimental import pallas as pl
from jax.experimental.pallas import tpu as pltpu

D_MODEL = 4096
FOX_WIDTH = 2048
FOX_HEAD_DIM = 128
FOX_HEADS = 16
RWKV_WIDTH = 2048
RWKV_HEAD_DIM = 64
RWKV_HEADS = 32
DECAY_LORA = 128
AAA_LORA = 128
GATE_LORA = 480
MEM_HEADS = 4
MEM_HEAD_DIM = 128
MEM_WIDTH = MEM_HEADS * MEM_HEAD_DIM
NORM_EPS = 1e-6
GN_EPS = 64e-5
FOX_QKVG = 4 * FOX_WIDTH
FOX_IN = FOX_QKVG + FOX_HEADS
RWKV_IN = 3 * RWKV_WIDTH + DECAY_LORA + AAA_LORA + GATE_LORA
R_OFF, WLO_OFF, K_OFF, V_OFF, ALO_OFF, GLO_OFF = 0, 2048, 2176, 4224, 6272, 6400

LANES = 128
SUBLANES = 8
VMEM_BYTES_V7X = 64 * 1024 * 1024
RWKV_READ = 6912
RWKV_PAD = 7168
GATE_PAD = RWKV_READ - GLO_OFF

LANE_GROUPS = LANES // RWKV_HEADS
PACK_ROWS = RWKV_WIDTH // LANES

F32 = jnp.float32
BF16 = jnp.bfloat16
HIGHEST = lax.Precision.HIGHEST


def _vmem_limit(block_bytes, scratch_bytes=0, temp_bytes=0):
    need = 2 * block_bytes + scratch_bytes + temp_bytes + (4 << 20)
    return int(min(need, VMEM_BYTES_V7X - (6 << 20)))


def _nbytes(shape, dtype):
    n = 1
    for s in shape:
        n *= s
    return n * jnp.dtype(dtype).itemsize


def _rmsnorm_kernel(x_ref, g_ref, o_ref):
    x = x_ref[...]
    ms = jnp.mean(x * x, axis=-1, keepdims=True)
    o_ref[...] = (x * lax.rsqrt(ms + NORM_EPS) * g_ref[...]).astype(o_ref.dtype)


def rmsnorm_bf16(x, gain, tm=256):
    m, d = x.shape
    tm = min(tm, m)
    return pl.pallas_call(
        _rmsnorm_kernel,
        grid=(m // tm,),
        in_specs=[pl.BlockSpec((tm, d), lambda i: (i, 0)),
                  pl.BlockSpec((1, d), lambda i: (0, 0))],
        out_specs=pl.BlockSpec((tm, d), lambda i: (i, 0)),
        out_shape=jax.ShapeDtypeStruct((m, d), BF16),
        compiler_params=pltpu.CompilerParams(
            dimension_semantics=("parallel",),
            vmem_limit_bytes=_vmem_limit(_nbytes((tm, d), F32) + _nbytes((tm, d), BF16),
                                         temp_bytes=2 * _nbytes((tm, d), F32))),
        name="rmsnorm",
    )(x, gain.reshape(1, d))


def _matmul_kernel(*refs, epilogue, nk):
    if epilogue == "residual":
        a_ref, w_ref, r_ref, o_ref = refs
    else:
        a_ref, w_ref, o_ref = refs

    def product():
        return jnp.dot(a_ref[...], w_ref[...].astype(BF16), preferred_element_type=F32)

    if nk == 1:
        d = product()
        if epilogue == "relu2":
            r = jnp.maximum(d, 0.0)
            o_ref[...] = (r * r).astype(o_ref.dtype)
        elif epilogue == "residual":
            o_ref[...] = r_ref[...] + d
        else:
            o_ref[...] = d.astype(o_ref.dtype)
    else:
        k = pl.program_id(2)

        @pl.when(k == 0)
        def _():
            o_ref[...] = r_ref[...] + product() if epilogue == "residual" else product()

        @pl.when(k > 0)
        def _():
            o_ref[...] += product()


def matmul(a, w, *, layer=None, n=None, epilogue="none", residual=None, out_dtype=F32,
           tm=1024, tn=512, tk=None):
    m, kdim = a.shape
    n = w.shape[-1] if n is None else n
    tk = kdim if tk is None else tk
    tm, tn = min(tm, m), min(tn, n)
    assert m % tm == 0 and n % tn == 0 and kdim % tk == 0, (a.shape, w.shape, tm, tn, tk)
    nk = kdim // tk
    assert nk == 1 or (out_dtype == F32 and epilogue != "relu2")
    if layer is None:
        w_spec = pl.BlockSpec((tk, tn), lambda i, j, k: (k, j))
    else:
        w_spec = pl.BlockSpec((None, tk, tn), lambda i, j, k: (layer, k, j))
    in_specs = [pl.BlockSpec((tm, tk), lambda i, j, k: (i, k)), w_spec]
    args = [a, w]
    blk = _nbytes((tm, tk), BF16) + _nbytes((tk, tn), w.dtype) + _nbytes((tm, tn), out_dtype)
    if epilogue == "residual":
        in_specs.append(pl.BlockSpec((tm, tn), lambda i, j, k: (i, j)))
        args.append(residual)
        blk += _nbytes((tm, tn), F32)
    return pl.pallas_call(
        functools.partial(_matmul_kernel, epilogue=epilogue, nk=nk),
        grid=(m // tm, n // tn, nk),
        in_specs=in_specs,
        out_specs=pl.BlockSpec((tm, tn), lambda i, j, k: (i, j)),
        out_shape=jax.ShapeDtypeStruct((m, n), out_dtype),
        compiler_params=pltpu.CompilerParams(
            dimension_semantics=("parallel", "parallel", "arbitrary"),
            vmem_limit_bytes=_vmem_limit(blk, temp_bytes=_nbytes((tk, tn), BF16) + 2 * _nbytes((tm, tn), F32))),
        name="matmul_" + epilogue,
    )(*args)


def _out_proj_kernel(a1_ref, a2_ref, w1_ref, w2_ref, r_ref, o_ref):
    d = jnp.dot(a1_ref[...], w1_ref[...].astype(BF16), preferred_element_type=F32)
    d = d + jnp.dot(a2_ref[...], w2_ref[...], preferred_element_type=F32)
    o_ref[...] = r_ref[...] + d


def out_proj(y_fox, y_rwkv, w_out, layer, w_rwkv_rows, x, tm=1024, tn=512):
    m, half = y_fox.shape
    n = x.shape[1]
    tm = min(tm, m)
    assert m % tm == 0 and n % tn == 0, (m, n, tm, tn)
    blk = (2 * _nbytes((tm, half), BF16) + _nbytes((half, tn), F32) + _nbytes((half, tn), BF16)
           + 2 * _nbytes((tm, tn), F32))
    return pl.pallas_call(
        _out_proj_kernel,
        grid=(m // tm, n // tn),
        in_specs=[pl.BlockSpec((tm, half), lambda i, j: (i, 0)),
                  pl.BlockSpec((tm, half), lambda i, j: (i, 0)),
                  pl.BlockSpec((None, half, tn), lambda i, j: (layer, 0, j)),
                  pl.BlockSpec((half, tn), lambda i, j: (0, j)),
                  pl.BlockSpec((tm, tn), lambda i, j: (i, j))],
        out_specs=pl.BlockSpec((tm, tn), lambda i, j: (i, j)),
        out_shape=jax.ShapeDtypeStruct((m, n), F32),
        compiler_params=pltpu.CompilerParams(
            dimension_semantics=("parallel", "parallel"),
            vmem_limit_bytes=_vmem_limit(blk, temp_bytes=_nbytes((half, tn), BF16) + 2 * _nbytes((tm, tn), F32))),
        name="out_proj",
    )(y_fox, y_rwkv, w_out, w_rwkv_rows, x)


def _fox_prep_kernel(p_ref, h_ref, wf_ref, fb_ref, qg_ref, kg_ref, tri_ref,
                     q_o, k_o, v_o, c_o, carry_ref, *, tq):
    @pl.when(pl.program_id(1) == 0)
    def _():
        carry_ref[...] = jnp.zeros_like(carry_ref)

    scale = FOX_HEAD_DIM ** -0.5
    for h in range(FOX_HEADS):
        lo = h * FOX_HEAD_DIM
        q = p_ref[:, lo:lo + FOX_HEAD_DIM]
        ms = jnp.mean(q * q, axis=-1, keepdims=True)
        q_o[:, lo:lo + FOX_HEAD_DIM] = (q * lax.rsqrt(ms + NORM_EPS) * qg_ref[...] * scale).astype(BF16)
        k = p_ref[:, FOX_WIDTH + lo:FOX_WIDTH + lo + FOX_HEAD_DIM]
        ms = jnp.mean(k * k, axis=-1, keepdims=True)
        k_o[:, lo:lo + FOX_HEAD_DIM] = (k * lax.rsqrt(ms + NORM_EPS) * kg_ref[...]).astype(BF16)
    v_o[...] = p_ref[:, 2 * FOX_WIDTH:3 * FOX_WIDTH].astype(BF16)

    f_logit = lax.dot_general(wf_ref[...], h_ref[...], (((1,), (1,)), ((), ())),
                              preferred_element_type=F32)
    log_f = jax.nn.log_sigmoid(f_logit + fb_ref[...])
    c = jnp.dot(log_f, tri_ref[...], precision=HIGHEST, preferred_element_type=F32) + carry_ref[...]
    c_o[...] = c
    carry_ref[...] = jnp.broadcast_to(c[:, tq - 1:tq], carry_ref.shape)


def fox_prep(pf3, h3, wf_t, f_bias, q_gain, k_gain, tq=512):
    b, s, _ = pf3.shape
    tq = min(tq, s)
    tri = (lax.broadcasted_iota(jnp.int32, (tq, tq), 0) <= lax.broadcasted_iota(jnp.int32, (tq, tq), 1)).astype(F32)
    qkv_w = 3 * FOX_WIDTH
    act = jax.ShapeDtypeStruct((b, s, FOX_WIDTH), BF16)
    blk = (_nbytes((tq, qkv_w), F32) + _nbytes((tq, D_MODEL), BF16) + _nbytes((FOX_HEADS, D_MODEL), BF16)
           + _nbytes((tq, tq), F32) + 3 * _nbytes((tq, FOX_WIDTH), BF16) + _nbytes((FOX_HEADS, tq), F32))
    return pl.pallas_call(
        functools.partial(_fox_prep_kernel, tq=tq),
        grid=(b, s // tq),
        in_specs=[pl.BlockSpec((None, tq, qkv_w), lambda bi, i: (bi, i, 0)),
                  pl.BlockSpec((None, tq, D_MODEL), lambda bi, i: (bi, i, 0)),
                  pl.BlockSpec((FOX_HEADS, D_MODEL), lambda bi, i: (0, 0)),
                  pl.BlockSpec((FOX_HEADS, 1), lambda bi, i: (0, 0)),
                  pl.BlockSpec((1, FOX_HEAD_DIM), lambda bi, i: (0, 0)),
                  pl.BlockSpec((1, FOX_HEAD_DIM), lambda bi, i: (0, 0)),
                  pl.BlockSpec((tq, tq), lambda bi, i: (0, 0))],
        out_specs=[pl.BlockSpec((None, tq, FOX_WIDTH), lambda bi, i: (bi, i, 0)),
                   pl.BlockSpec((None, tq, FOX_WIDTH), lambda bi, i: (bi, i, 0)),
                   pl.BlockSpec((None, tq, FOX_WIDTH), lambda bi, i: (bi, i, 0)),
                   pl.BlockSpec((None, FOX_HEADS, tq), lambda bi, i: (bi, 0, i))],
        out_shape=[act, act, act, jax.ShapeDtypeStruct((b, FOX_HEADS, s), F32)],
        scratch_shapes=[pltpu.VMEM((FOX_HEADS, tq), F32)],
        compiler_params=pltpu.CompilerParams(
            dimension_semantics=("parallel", "arbitrary"),
            vmem_limit_bytes=_vmem_limit(blk, temp_bytes=_nbytes((tq, qkv_w), F32))),
        name="fox_prep",
    )(pf3, h3, wf_t, f_bias.reshape(FOX_HEADS, 1), q_gain.reshape(1, -1), k_gain.reshape(1, -1), tri)


def _fox_attn_kernel(q_ref, k_ref, v_ref, c_ref, gate_ref, o_ref, m_ref, l_ref, acc_ref, *, t):
    qi = pl.program_id(2)
    m_ref[...] = jnp.full_like(m_ref, -jnp.inf)
    l_ref[...] = jnp.zeros_like(l_ref)
    acc_ref[...] = jnp.zeros_like(acc_ref)
    q = q_ref[...]
    c_base = c_ref[pl.ds(qi, 1), :][:, 0:1]

    def block(ki, masked):
        start = pl.multiple_of(ki * t, t)
        k = k_ref[pl.ds(start, t), :]
        v = v_ref[pl.ds(start, t), :]
        s = lax.dot_general(q, k, (((1,), (1,)), ((), ())), preferred_element_type=F32)
        s = s + (c_base - c_ref[pl.ds(ki, 1), :])
        if masked:
            row = lax.broadcasted_iota(jnp.int32, (t, t), 0)
            col = lax.broadcasted_iota(jnp.int32, (t, t), 1)
            s = jnp.where(col <= row, s, -jnp.inf)
        m_prev = m_ref[...]
        m_new = jnp.maximum(m_prev, jnp.max(s, axis=1, keepdims=True))
        alpha = jnp.exp(m_prev - m_new)
        p = jnp.exp(s - jnp.tile(m_new, (1, t // LANES)))
        l_ref[...] = alpha * l_ref[...] + jnp.sum(p, axis=1, keepdims=True)
        acc_ref[...] = alpha * acc_ref[...] + jnp.dot(p.astype(BF16), v, preferred_element_type=F32)
        m_ref[...] = m_new

    def body(ki, carry):
        block(ki, False)
        return carry

    lax.fori_loop(0, qi, body, 0)
    block(qi, True)
    o = acc_ref[...] / l_ref[...]
    o_ref[...] = (o * jax.nn.sigmoid(gate_ref[...])).astype(o_ref.dtype)


def fox_attention(qn, kn, vb, c, pf3, t=512):
    b, s, _ = qn.shape
    t = min(t, s)
    nt = s // t
    c4 = c.reshape(b, FOX_HEADS, nt, t)
    gate_blk0 = 3 * FOX_WIDTH // FOX_HEAD_DIM
    blk = (2 * _nbytes((t, LANES), BF16) + 2 * _nbytes((s, LANES), BF16) + _nbytes((nt, t), F32)
           + _nbytes((t, LANES), F32))
    return pl.pallas_call(
        functools.partial(_fox_attn_kernel, t=t),
        grid=(b, FOX_HEADS, nt),
        in_specs=[pl.BlockSpec((None, t, FOX_HEAD_DIM), lambda bi, h, qi: (bi, qi, h)),
                  pl.BlockSpec((None, s, FOX_HEAD_DIM), lambda bi, h, qi: (bi, 0, h)),
                  pl.BlockSpec((None, s, FOX_HEAD_DIM), lambda bi, h, qi: (bi, 0, h)),
                  pl.BlockSpec((None, None, nt, t), lambda bi, h, qi: (bi, h, 0, 0)),
                  pl.BlockSpec((None, t, FOX_HEAD_DIM), lambda bi, h, qi: (bi, qi, gate_blk0 + h))],
        out_specs=pl.BlockSpec((None, t, FOX_HEAD_DIM), lambda bi, h, qi: (bi, qi, h)),
        out_shape=jax.ShapeDtypeStruct((b, s, FOX_WIDTH), BF16),
        scratch_shapes=[pltpu.VMEM((t, LANES), F32), pltpu.VMEM((t, LANES), F32),
                        pltpu.VMEM((t, FOX_HEAD_DIM), F32)],
        compiler_params=pltpu.CompilerParams(
            dimension_semantics=("parallel", "parallel", "arbitrary"),
            vmem_limit_bytes=_vmem_limit(blk, scratch_bytes=3 * _nbytes((t, LANES), F32),
                                         temp_bytes=6 * _nbytes((t, t), F32))),
        name="fox_attention",
    )(qn, kn, vb, c4, pf3)


def _to_column_layout(x, axis=-1):
    x = jnp.moveaxis(x, axis, -1)
    y = x.reshape(x.shape[:-1] + (RWKV_HEADS, RWKV_HEAD_DIM)).swapaxes(-1, -2).reshape(x.shape)
    return jnp.moveaxis(y, -1, axis)


def _to_value_layout(x, axis=-1):
    x = jnp.moveaxis(x, axis, -1)
    lead = x.ndim - 1
    y = x.reshape(x.shape[:-1] + (RWKV_HEADS, 2, LANE_GROUPS, SUBLANES))
    y = y.transpose(tuple(range(lead)) + (lead + 1, lead + 3, lead + 2, lead)).reshape(x.shape)
    return jnp.moveaxis(y, -1, axis)


def _group_allreduce(x, axis):
    x = x + pltpu.roll(x, 2 * RWKV_HEADS, axis=axis)
    return x + pltpu.roll(x, RWKV_HEADS, axis=axis)


def _head_sum(x):
    s = x[:, 0:LANES]
    for row in range(1, PACK_ROWS):
        s = s + x[:, row * LANES:(row + 1) * LANES]
    return _group_allreduce(s, 1)


def _rwkv_prep_kernel(p_ref, pprev_ref, mu_ref, w0_ref, w2_ref, a0_ref, a2_ref, g2_ref, kk_ref, ka_ref,
                      rk_ref, r_o, w_o, k_o, v_o, a_o, b_o, g_o, bonus_o, *, tiles_per_seq):
    first = (pl.program_id(0) % tiles_per_seq) == 0

    def shifted(lo, hi):
        p = p_ref[:, lo:hi]
        prev_row = jnp.where(first, 0.0, pprev_ref[SUBLANES - 1:SUBLANES, lo:hi])
        row = lax.broadcasted_iota(jnp.int32, p.shape, 0)
        prev = jnp.where(row == 0, prev_row, pltpu.roll(p, 1, axis=0))
        return p + (prev - p) * mu_ref[:, lo:hi]

    def over_rows(s):
        return jnp.tile(s, (1, PACK_ROWS))

    w_lo = shifted(WLO_OFF, WLO_OFF + DECAY_LORA)
    z = w0_ref[...] + jnp.dot(jnp.tanh(w_lo), w2_ref[...], precision=HIGHEST, preferred_element_type=F32)
    softplus_neg = jnp.maximum(-z, 0.0) + jnp.log1p(jnp.exp(-jnp.abs(z)))
    w_o[...] = jnp.exp(-jnp.exp(-softplus_neg - 0.5))

    a_lo = shifted(ALO_OFF, ALO_OFF + AAA_LORA)
    a_lr = jax.nn.sigmoid(a0_ref[...] + jnp.dot(a_lo, a2_ref[...], precision=HIGHEST,
                                                preferred_element_type=F32))

    g_lo = shifted(GLO_OFF, GLO_OFF + GATE_PAD)
    g_o[...] = jnp.dot(jax.nn.sigmoid(g_lo).astype(BF16), g2_ref[...], preferred_element_type=F32)

    k = shifted(K_OFF, K_OFF + RWKV_WIDTH)
    kk = k * kk_ref[...]
    kk = kk * over_rows(lax.rsqrt(jnp.maximum(_head_sum(kk * kk), 1e-24)))
    a_o[...] = -kk
    b_o[...] = kk * a_lr
    k = k * (1.0 + (a_lr - 1.0) * ka_ref[...])
    k_o[...] = k

    r = shifted(R_OFF, R_OFF + RWKV_WIDTH)
    r_o[...] = r
    v = shifted(V_OFF, V_OFF + RWKV_WIDTH)
    v_o[...] = v
    bonus_o[...] = over_rows(_head_sum(r * k * rk_ref[...])) * v


def rwkv_prep(pr, seq, mu, w0, w2, a0, a2, g2p, k_k, k_a, r_k, tq=128):
    t_total = pr.shape[0]
    width = RWKV_READ
    tq = min(tq, seq)
    row = lambda x: x.reshape(1, -1)
    full = lambda shape: pl.BlockSpec(shape, lambda i: (0, 0))
    out = jax.ShapeDtypeStruct((t_total, RWKV_WIDTH), F32)
    out_spec = pl.BlockSpec((tq, RWKV_WIDTH), lambda i: (i, 0))
    sub_per_tile = tq // SUBLANES
    blk = (_nbytes((tq, width), F32) + _nbytes((SUBLANES, width), F32) + 8 * _nbytes((tq, RWKV_WIDTH), F32)
           + _nbytes((DECAY_LORA + AAA_LORA, RWKV_WIDTH), F32) + _nbytes((GATE_PAD, RWKV_WIDTH), BF16))
    return pl.pallas_call(
        functools.partial(_rwkv_prep_kernel, tiles_per_seq=seq // tq),
        grid=(t_total // tq,),
        in_specs=[pl.BlockSpec((tq, width), lambda i: (i, 0)),
                  pl.BlockSpec((SUBLANES, width), lambda i: (jnp.maximum(i * sub_per_tile - 1, 0), 0)),
                  full((1, width)), full((1, RWKV_WIDTH)), full((DECAY_LORA, RWKV_WIDTH)),
                  full((1, RWKV_WIDTH)), full((AAA_LORA, RWKV_WIDTH)), full((GATE_PAD, RWKV_WIDTH)),
                  full((1, RWKV_WIDTH)), full((1, RWKV_WIDTH)), full((1, RWKV_WIDTH))],
        out_specs=[out_spec] * 8,
        out_shape=[out] * 8,
        compiler_params=pltpu.CompilerParams(
            dimension_semantics=("parallel",),
            vmem_limit_bytes=_vmem_limit(blk, temp_bytes=12 * _nbytes((tq, RWKV_WIDTH), F32))),
        name="rwkv_prep",
    )(pr, pr, row(mu), row(w0), w2, row(a0), a2, g2p, row(k_k), row(k_a), row(r_k))


V_TILES = RWKV_HEAD_DIM // SUBLANES


def _rwkv_scan_kernel(r_ref, w_ref, k_ref, v_ref, a_ref, b_ref, bonus_ref, g_ref, gain_ref, bias_ref, o_ref,
                      state_ref, vrep_ref, y_ref, *, tc, nb):
    @pl.when(pl.program_id(0) == 0)
    def _():
        state_ref[...] = jnp.zeros_like(state_ref)

    lane_group = lax.broadcasted_iota(jnp.int32, (1, 1, LANES), 2) // RWKV_HEADS

    def by_lane_group(pick):
        out = pick(LANE_GROUPS - 1)
        for grp in range(LANE_GROUPS - 2, -1, -1):
            out = jnp.where(lane_group == grp, pick(grp), out)
        return out

    for b in range(nb):
        for gp in range(2):
            x = v_ref[b, :, gp * SUBLANES:(gp + 1) * SUBLANES, :]
            rolled = [x] + [pltpu.roll(x, j * RWKV_HEADS, axis=2) for j in range(1, LANE_GROUPS)]
            for q in range(LANE_GROUPS):
                tile = gp * LANE_GROUPS + q
                vrep_ref[b, :, tile * SUBLANES:(tile + 1) * SUBLANES, :] = by_lane_group(
                    lambda grp: rolled[(grp - q) % LANE_GROUPS])

    def bcast(ref, b, t, row):
        return jnp.broadcast_to(ref[b, t, row:row + 1, :], (SUBLANES, LANES))

    def tile(g):
        return pl.ds(g * SUBLANES, SUBLANES)

    def step(t, carry):
        for b in range(nb):
            sa = [None] * V_TILES
            for row in range(PACK_ROWS):
                ab = bcast(a_ref, b, t, row)
                for g in range(V_TILES):
                    term = state_ref[b, row, tile(g), :] * ab
                    sa[g] = term if row == 0 else sa[g] + term
            sa = [_group_allreduce(x, 1) for x in sa]
            vt = [vrep_ref[b, t, tile(g), :] for g in range(V_TILES)]
            y = [None] * V_TILES
            for row in range(PACK_ROWS):
                wb, bb = bcast(w_ref, b, t, row), bcast(b_ref, b, t, row)
                kb, rb = bcast(k_ref, b, t, row), bcast(r_ref, b, t, row)
                for g in range(V_TILES):
                    s = state_ref[b, row, tile(g), :] * wb + sa[g] * bb + vt[g] * kb
                    state_ref[b, row, tile(g), :] = s
                    y[g] = s * rb if row == 0 else y[g] + s * rb
            for g in range(V_TILES):
                y_ref[b, t, tile(g), :] = y[g]
        return carry

    lax.fori_loop(0, tc, step, 0)

    inv_n = 1.0 / RWKV_HEAD_DIM
    for b in range(nb):
        halves = []
        for gp in range(2):
            tiles = [y_ref[b, :, (gp * LANE_GROUPS + q) * SUBLANES:(gp * LANE_GROUPS + q + 1) * SUBLANES, :]
                     for q in range(LANE_GROUPS)]
            packed = by_lane_group(lambda grp: tiles[grp])
            for j in range(1, LANE_GROUPS):
                m_j = by_lane_group(lambda grp: tiles[(grp + j) % LANE_GROUPS])
                packed = packed + pltpu.roll(m_j, j * RWKV_HEADS, axis=2)
            halves.append(packed)
        y = jnp.concatenate(halves, axis=1)
        mean = _group_allreduce(jnp.sum(y, axis=1, keepdims=True), 2) * inv_n
        yc = y - mean
        var = _group_allreduce(jnp.sum(yc * yc, axis=1, keepdims=True), 2) * inv_n
        yn = yc * lax.rsqrt(var + GN_EPS)
        out = (yn * gain_ref[...] + bias_ref[...] + bonus_ref[b]) * g_ref[b]
        o_ref[b] = out.astype(o_ref.dtype)


def rwkv_scan(r, w, k, v, a, b, bonus, g, gn_gain, gn_bias, batch, tc=32):
    t_total = r.shape[0]
    s = t_total // batch
    tc = min(tc, s)
    packed = lambda x: x.reshape(batch, s, PACK_ROWS, LANES)
    spec = pl.BlockSpec((batch, tc, PACK_ROWS, LANES), lambda i: (0, i, 0, 0))
    affine = pl.BlockSpec((PACK_ROWS, LANES), lambda i: (0, 0))
    blk = 8 * _nbytes((batch, tc, PACK_ROWS, LANES), F32) + _nbytes((batch, tc, PACK_ROWS, LANES), BF16)
    slab = _nbytes((batch, tc, RWKV_HEAD_DIM, LANES), F32)
    out = pl.pallas_call(
        functools.partial(_rwkv_scan_kernel, tc=tc, nb=batch),
        grid=(s // tc,),
        in_specs=[spec] * 8 + [affine, affine],
        out_specs=spec,
        out_shape=jax.ShapeDtypeStruct((batch, s, PACK_ROWS, LANES), BF16),
        scratch_shapes=[pltpu.VMEM((batch, PACK_ROWS, RWKV_HEAD_DIM, LANES), F32),
                        pltpu.VMEM((batch, tc, RWKV_HEAD_DIM, LANES), F32),
                        pltpu.VMEM((batch, tc, RWKV_HEAD_DIM, LANES), F32)],
        compiler_params=pltpu.CompilerParams(
            dimension_semantics=("arbitrary",),
            vmem_limit_bytes=_vmem_limit(blk, scratch_bytes=2 * slab + _nbytes((batch, PACK_ROWS, RWKV_HEAD_DIM, LANES), F32),
                                         temp_bytes=6 * slab)),
        name="rwkv_scan",
    )(packed(r), packed(w), packed(k), packed(v), packed(a), packed(b), packed(bonus), packed(g),
      gn_gain.reshape(PACK_ROWS, LANES), gn_bias.reshape(PACK_ROWS, LANES))
    return out.reshape(t_total, RWKV_WIDTH)


def _mem_attn_kernel(q_ref, k_ref, v_ref, qg_ref, kg_ref, o_ref):
    scale = MEM_HEAD_DIM ** -0.5
    for h in range(MEM_HEADS):
        cols = slice(h * MEM_HEAD_DIM, (h + 1) * MEM_HEAD_DIM)
        q = q_ref[:, cols]
        q = q * lax.rsqrt(jnp.mean(q * q, axis=-1, keepdims=True) + NORM_EPS) * qg_ref[...]
        k = k_ref[:, cols]
        k = k * lax.rsqrt(jnp.mean(k * k, axis=-1, keepdims=True) + NORM_EPS) * kg_ref[...]
        s = lax.dot_general(q.astype(BF16), k.astype(BF16), (((1,), (1,)), ((), ())),
                            preferred_element_type=F32) * scale
        p = jnp.exp(s - jnp.max(s, axis=-1, keepdims=True))
        p = p / jnp.sum(p, axis=-1, keepdims=True)
        o = jnp.dot(p.astype(BF16), v_ref[:, cols].astype(BF16), preferred_element_type=F32)
        o_ref[:, cols] = o.astype(o_ref.dtype)


def mem_attention(q3, k3, v3, q_gain, k_gain, tq=512):
    b, s, _ = q3.shape
    m = k3.shape[1]
    tq = min(tq, s)
    blk = _nbytes((tq, MEM_WIDTH), F32) + 2 * _nbytes((m, MEM_WIDTH), F32) + _nbytes((tq, MEM_WIDTH), BF16)
    return pl.pallas_call(
        _mem_attn_kernel,
        grid=(b, s // tq),
        in_specs=[pl.BlockSpec((None, tq, MEM_WIDTH), lambda bi, i: (bi, i, 0)),
                  pl.BlockSpec((None, m, MEM_WIDTH), lambda bi, i: (bi, 0, 0)),
                  pl.BlockSpec((None, m, MEM_WIDTH), lambda bi, i: (bi, 0, 0)),
                  pl.BlockSpec((1, MEM_HEAD_DIM), lambda bi, i: (0, 0)),
                  pl.BlockSpec((1, MEM_HEAD_DIM), lambda bi, i: (0, 0))],
        out_specs=pl.BlockSpec((None, tq, MEM_WIDTH), lambda bi, i: (bi, i, 0)),
        out_shape=jax.ShapeDtypeStruct((b, s, MEM_WIDTH), BF16),
        compiler_params=pltpu.CompilerParams(
            dimension_semantics=("parallel", "parallel"),
            vmem_limit_bytes=_vmem_limit(blk, temp_bytes=8 * _nbytes((tq, m), F32))),
        name="mem_attention",
    )(q3, k3, v3, q_gain.reshape(1, -1), k_gain.reshape(1, -1))


def _rwkv_projection_weight(w_in):
    w = w_in[:, FOX_IN:]
    parts = [_to_column_layout(w[:, R_OFF:WLO_OFF]), w[:, WLO_OFF:K_OFF], _to_column_layout(w[:, K_OFF:V_OFF]),
             _to_value_layout(w[:, V_OFF:ALO_OFF]), w[:, ALO_OFF:],
             jnp.zeros((w.shape[0], RWKV_PAD - RWKV_IN), w.dtype)]
    return jnp.concatenate(parts, axis=1).astype(BF16)


def _rwkv_shift_mix(mu):
    parts = [_to_column_layout(mu[R_OFF:WLO_OFF]), mu[WLO_OFF:K_OFF], _to_column_layout(mu[K_OFF:V_OFF]),
             _to_value_layout(mu[V_OFF:ALO_OFF]), mu[ALO_OFF:], jnp.zeros((RWKV_READ - RWKV_IN,), mu.dtype)]
    return jnp.concatenate(parts)


def _layer(x, mem2, b, s, layer, p, w):
    t_total = b * s
    h = rmsnorm_bf16(x, p["norm_mix"])
    pf = matmul(h, w["w_in"], layer=layer, n=FOX_QKVG)
    pr = matmul(h, _rwkv_projection_weight(w["w_in"][layer]))

    pf3 = pf.reshape(b, s, FOX_QKVG)
    wf_t = w["w_in"][layer][:, FOX_QKVG:FOX_IN].T.astype(BF16)
    qn, kn, vb, c = fox_prep(pf3, h.reshape(b, s, D_MODEL), wf_t, p["fox_f_bias"],
                             p["fox_q_gain"], p["fox_k_gain"])
    y_fox = fox_attention(qn, kn, vb, c, pf3).reshape(t_total, FOX_WIDTH)

    g2p = _to_value_layout(jnp.pad(p["rwkv_g2"], ((0, GATE_PAD - GATE_LORA), (0, 0)))).astype(BF16)
    r, wd, k, v, a, bb, g, bonus = rwkv_prep(
        pr, s, _rwkv_shift_mix(p["rwkv_mu"]), _to_column_layout(p["rwkv_w0"]), _to_column_layout(p["rwkv_w2"]),
        _to_column_layout(p["rwkv_a0"]), _to_column_layout(p["rwkv_a2"]), g2p, _to_column_layout(p["rwkv_k_k"]),
        _to_column_layout(p["rwkv_k_a"]), _to_column_layout(p["rwkv_r_k"].reshape(-1)))
    y_rwkv = rwkv_scan(r, wd, k, v, a, bb, bonus, g, _to_value_layout(p["rwkv_gn_gain"]),
                       _to_value_layout(p["rwkv_gn_bias"]), b)

    w_rwkv_rows = _to_value_layout(w["w_out"][layer][FOX_WIDTH:], axis=0).astype(BF16)
    x = out_proj(y_fox, y_rwkv, w["w_out"], layer, w_rwkv_rows, x)

    h = rmsnorm_bf16(x, p["norm_mem_q"])
    m = rmsnorm_bf16(mem2, p["norm_mem_kv"])
    q = matmul(h, w["mem_w_q"], layer=layer)
    km = matmul(m, w["mem_w_k"], layer=layer)
    vm = matmul(m, w["mem_w_v"], layer=layer)
    n_mem = mem2.shape[0] // b
    o = mem_attention(q.reshape(b, s, MEM_WIDTH), km.reshape(b, n_mem, MEM_WIDTH),
                      vm.reshape(b, n_mem, MEM_WIDTH), p["mem_q_gain"], p["mem_k_gain"])
    x = matmul(o.reshape(t_total, MEM_WIDTH), w["mem_w_o"], layer=layer, epilogue="residual", residual=x,
               tn=1024)

    h = rmsnorm_bf16(x, p["norm_mlp"])
    u = matmul(h, w["w_up"], layer=layer, epilogue="relu2", out_dtype=BF16)
    x = matmul(u, w["w_down"], layer=layer, epilogue="residual", residual=x, tm=1024, tn=1024, tk=2048)
    return x


_PARAM_NAMES = ("norm_mix", "w_in", "fox_q_gain", "fox_k_gain", "fox_f_bias", "rwkv_mu", "rwkv_w0", "rwkv_w2",
                "rwkv_a0", "rwkv_a2", "rwkv_g2", "rwkv_k_k", "rwkv_k_a", "rwkv_r_k", "rwkv_gn_gain",
                "rwkv_gn_bias", "w_out", "norm_mem_q", "norm_mem_kv", "mem_w_q", "mem_w_k", "mem_w_v",
                "mem_q_gain", "mem_k_gain", "mem_w_o", "norm_mlp", "w_up", "w_down")
_STACKED_WEIGHTS = ("w_in", "w_out", "mem_w_q", "mem_w_k", "mem_w_v", "mem_w_o", "w_up", "w_down")


def kernel(x, mem, norm_mix, w_in, fox_q_gain, fox_k_gain, fox_f_bias, rwkv_mu, rwkv_w0, rwkv_w2, rwkv_a0, rwkv_a2, rwkv_g2, rwkv_k_k, rwkv_k_a, rwkv_r_k, rwkv_gn_gain, rwkv_gn_bias, w_out, norm_mem_q, norm_mem_kv, mem_w_q, mem_w_k, mem_w_v, mem_q_gain, mem_k_gain, mem_w_o, norm_mlp, w_up, w_down):
    params = dict(zip(_PARAM_NAMES, (norm_mix, w_in, fox_q_gain, fox_k_gain, fox_f_bias, rwkv_mu, rwkv_w0,
                                     rwkv_w2, rwkv_a0, rwkv_a2, rwkv_g2, rwkv_k_k, rwkv_k_a, rwkv_r_k,
                                     rwkv_gn_gain, rwkv_gn_bias, w_out, norm_mem_q, norm_mem_kv, mem_w_q,
                                     mem_w_k, mem_w_v, mem_q_gain, mem_k_gain, mem_w_o, norm_mlp, w_up, w_down)))
    b, s, d = x.shape
    assert d == D_MODEL, x.shape
    depth = w_in.shape[0]
    stacked = {name: params[name] for name in _STACKED_WEIGHTS}
    x2 = x.reshape(b * s, d)
    mem2 = mem.reshape(-1, d)
    for layer in range(depth):
        small = {name: value[layer] for name, value in params.items() if name not in _STACKED_WEIGHTS}
        x2 = _layer(x2, mem2, b, s, layer, small, stacked)
    return x2.reshape(b, s, d)
```

```python
import functools

import jax
import jax.numpy as jnp
from jax import lax
from jax.experimental import pallas as pl
from jax.experimental.pallas import tpu as pltpu

D_MODEL = 4096
FOX_WIDTH = 2048
FOX_HEAD_DIM = 128
FOX_HEADS = 16
RWKV_WIDTH = 2048
RWKV_HEAD_DIM = 64
RWKV_HEADS = 32
DECAY_LORA = 128
AAA_LORA = 128
GATE_LORA = 480
MEM_HEADS = 4
MEM_HEAD_DIM = 128
MEM_WIDTH = MEM_HEADS * MEM_HEAD_DIM
NORM_EPS = 1e-6
GN_EPS = 64e-5
FOX_QKVG = 4 * FOX_WIDTH
FOX_IN = FOX_QKVG + FOX_HEADS
RWKV_IN = 3 * RWKV_WIDTH + DECAY_LORA + AAA_LORA + GATE_LORA
R_OFF, WLO_OFF, K_OFF, V_OFF, ALO_OFF, GLO_OFF = 0, 2048, 2176, 4224, 6272, 6400

LANES = 128
SUBLANES = 8
VMEM_BYTES_V7X = 64 * 1024 * 1024
RWKV_READ = 6912
RWKV_PAD = 7168
GATE_PAD = RWKV_READ - GLO_OFF

LANE_GROUPS = LANES // RWKV_HEADS
PACK_ROWS = RWKV_WIDTH // LANES

F32 = jnp.float32
BF16 = jnp.bfloat16
HIGHEST = lax.Precision.HIGHEST
LOG2_E = 1.4426950408889634


def _vmem_limit(block_bytes, scratch_bytes=0, temp_bytes=0):
    need = 2 * block_bytes + scratch_bytes + temp_bytes + (4 << 20)
    return int(min(need, VMEM_BYTES_V7X - (6 << 20)))


def _nbytes(shape, dtype):
    n = 1
    for s in shape:
        n *= s
    return n * jnp.dtype(dtype).itemsize


def _rmsnorm_kernel(x_ref, g_ref, o_ref):
    x = x_ref[...]
    ms = jnp.mean(x * x, axis=-1, keepdims=True)
    o_ref[...] = (x * lax.rsqrt(ms + NORM_EPS) * g_ref[...]).astype(o_ref.dtype)


def rmsnorm_bf16(x, gain, tm=256):
    m, d = x.shape
    tm = min(tm, m)
    return pl.pallas_call(
        _rmsnorm_kernel,
        grid=(m // tm,),
        in_specs=[pl.BlockSpec((tm, d), lambda i: (i, 0)),
                  pl.BlockSpec((1, d), lambda i: (0, 0))],
        out_specs=pl.BlockSpec((tm, d), lambda i: (i, 0)),
        out_shape=jax.ShapeDtypeStruct((m, d), BF16),
        compiler_params=pltpu.CompilerParams(
            dimension_semantics=("parallel",),
            vmem_limit_bytes=_vmem_limit(_nbytes((tm, d), F32) + _nbytes((tm, d), BF16),
                                         temp_bytes=2 * _nbytes((tm, d), F32))),
        name="rmsnorm",
    )(x, gain.reshape(1, d))


def _matmul_kernel(*refs, epilogue, nk):
    if epilogue == "residual":
        a_ref, w_ref, r_ref, o_ref = refs
    else:
        a_ref, w_ref, o_ref = refs

    def product():
        return jnp.dot(a_ref[...], w_ref[...].astype(BF16), preferred_element_type=F32)

    if nk == 1:
        d = product()
        if epilogue == "relu2":
            r = jnp.maximum(d, 0.0)
            o_ref[...] = (r * r).astype(o_ref.dtype)
        elif epilogue == "residual":
            o_ref[...] = r_ref[...] + d
        else:
            o_ref[...] = d.astype(o_ref.dtype)
    else:
        k = pl.program_id(2)

        @pl.when(k == 0)
        def _():
            o_ref[...] = r_ref[...] + product() if epilogue == "residual" else product()

        @pl.when(k > 0)
        def _():
            o_ref[...] += product()


def matmul(a, w, *, layer=None, n=None, epilogue="none", residual=None, out_dtype=F32,
           tm=1024, tn=512, tk=None, single_buffer_a=False):
    m, kdim = a.shape
    n = w.shape[-1] if n is None else n
    tk = kdim if tk is None else tk
    tm, tn = min(tm, m), min(tn, n)
    assert m % tm == 0 and n % tn == 0 and kdim % tk == 0, (a.shape, w.shape, tm, tn, tk)
    nk = kdim // tk
    assert nk == 1 or (out_dtype == F32 and epilogue != "relu2")
    if layer is None:
        w_spec = pl.BlockSpec((tk, tn), lambda i, j, k: (k, j))
    else:
        w_spec = pl.BlockSpec((None, tk, tn), lambda i, j, k: (layer, k, j))
    a_mode = pl.Buffered(1) if single_buffer_a else None
    in_specs = [pl.BlockSpec((tm, tk), lambda i, j, k: (i, k), pipeline_mode=a_mode), w_spec]
    args = [a, w]
    a_bytes = _nbytes((tm, tk), BF16)
    blk = (a_bytes // 2 if single_buffer_a else a_bytes) + _nbytes((tk, tn), w.dtype) + _nbytes((tm, tn), out_dtype)
    if epilogue == "residual":
        in_specs.append(pl.BlockSpec((tm, tn), lambda i, j, k: (i, j)))
        args.append(residual)
        blk += _nbytes((tm, tn), F32)
    return pl.pallas_call(
        functools.partial(_matmul_kernel, epilogue=epilogue, nk=nk),
        grid=(m // tm, n // tn, nk),
        in_specs=in_specs,
        out_specs=pl.BlockSpec((tm, tn), lambda i, j, k: (i, j)),
        out_shape=jax.ShapeDtypeStruct((m, n), out_dtype),
        compiler_params=pltpu.CompilerParams(
            dimension_semantics=("parallel", "parallel", "arbitrary"),
            vmem_limit_bytes=_vmem_limit(blk, temp_bytes=_nbytes((tk, tn), BF16) + 2 * _nbytes((tm, tn), F32))),
        name="matmul_" + epilogue,
    )(*args)


def _out_proj_kernel(a1_ref, a2_ref, w1_ref, w2_ref, r_ref, o_ref):
    d = jnp.dot(a1_ref[...], w1_ref[...].astype(BF16), preferred_element_type=F32)
    d = d + jnp.dot(a2_ref[...], w2_ref[...], preferred_element_type=F32)
    o_ref[...] = r_ref[...] + d


def out_proj(y_fox, y_rwkv, w_out, layer, w_rwkv_rows, x, tm=1024, tn=512):
    m, half = y_fox.shape
    n = x.shape[1]
    tm = min(tm, m)
    assert m % tm == 0 and n % tn == 0, (m, n, tm, tn)
    blk = (2 * _nbytes((tm, half), BF16) + _nbytes((half, tn), F32) + _nbytes((half, tn), BF16)
           + 2 * _nbytes((tm, tn), F32))
    return pl.pallas_call(
        _out_proj_kernel,
        grid=(m // tm, n // tn),
        in_specs=[pl.BlockSpec((tm, half), lambda i, j: (i, 0)),
                  pl.BlockSpec((tm, half), lambda i, j: (i, 0)),
                  pl.BlockSpec((None, half, tn), lambda i, j: (layer, 0, j)),
                  pl.BlockSpec((half, tn), lambda i, j: (0, j)),
                  pl.BlockSpec((tm, tn), lambda i, j: (i, j))],
        out_specs=pl.BlockSpec((tm, tn), lambda i, j: (i, j)),
        out_shape=jax.ShapeDtypeStruct((m, n), F32),
        compiler_params=pltpu.CompilerParams(
            dimension_semantics=("parallel", "parallel"),
            vmem_limit_bytes=_vmem_limit(blk, temp_bytes=_nbytes((half, tn), BF16) + 2 * _nbytes((tm, tn), F32))),
        name="out_proj",
    )(y_fox, y_rwkv, w_out, w_rwkv_rows, x)


def _fox_prep_kernel(p_ref, h_ref, wf_ref, fb_ref, qg_ref, kg_ref, tri_ref,
                     q_o, k_o, v_o, c_o, carry_ref, *, tq):
    @pl.when(pl.program_id(1) == 0)
    def _():
        carry_ref[...] = jnp.zeros_like(carry_ref)

    scale = FOX_HEAD_DIM ** -0.5 * LOG2_E
    for h in range(FOX_HEADS):
        lo = h * FOX_HEAD_DIM
        q = p_ref[:, lo:lo + FOX_HEAD_DIM]
        ms = jnp.mean(q * q, axis=-1, keepdims=True)
        q_o[:, lo:lo + FOX_HEAD_DIM] = (q * lax.rsqrt(ms + NORM_EPS) * qg_ref[...] * scale).astype(BF16)
        k = p_ref[:, FOX_WIDTH + lo:FOX_WIDTH + lo + FOX_HEAD_DIM]
        ms = jnp.mean(k * k, axis=-1, keepdims=True)
        k_o[:, lo:lo + FOX_HEAD_DIM] = (k * lax.rsqrt(ms + NORM_EPS) * kg_ref[...]).astype(BF16)
    v_o[...] = p_ref[:, 2 * FOX_WIDTH:3 * FOX_WIDTH].astype(BF16)

    f_logit = lax.dot_general(wf_ref[...], h_ref[...], (((1,), (1,)), ((), ())),
                              preferred_element_type=F32)
    log_f = jax.nn.log_sigmoid(f_logit + fb_ref[...])
    c = jnp.dot(log_f, tri_ref[...], precision=HIGHEST, preferred_element_type=F32) + carry_ref[...]
    c_o[...] = c * LOG2_E
    carry_ref[...] = jnp.broadcast_to(c[:, tq - 1:tq], carry_ref.shape)


def fox_prep(pf3, h3, wf_t, f_bias, q_gain, k_gain, tq=512):
    b, s, _ = pf3.shape
    tq = min(tq, s)
    tri = (lax.broadcasted_iota(jnp.int32, (tq, tq), 0) <= lax.broadcasted_iota(jnp.int32, (tq, tq), 1)).astype(F32)
    qkv_w = 3 * FOX_WIDTH
    act = jax.ShapeDtypeStruct((b, s, FOX_WIDTH), BF16)
    blk = (_nbytes((tq, qkv_w), F32) + _nbytes((tq, D_MODEL), BF16) + _nbytes((FOX_HEADS, D_MODEL), BF16)
           + _nbytes((tq, tq), F32) + 3 * _nbytes((tq, FOX_WIDTH), BF16) + _nbytes((FOX_HEADS, tq), F32))
    return pl.pallas_call(
        functools.partial(_fox_prep_kernel, tq=tq),
        grid=(b, s // tq),
        in_specs=[pl.BlockSpec((None, tq, qkv_w), lambda bi, i: (bi, i, 0)),
                  pl.BlockSpec((None, tq, D_MODEL), lambda bi, i: (bi, i, 0)),
                  pl.BlockSpec((FOX_HEADS, D_MODEL), lambda bi, i: (0, 0)),
                  pl.BlockSpec((FOX_HEADS, 1), lambda bi, i: (0, 0)),
                  pl.BlockSpec((1, FOX_HEAD_DIM), lambda bi, i: (0, 0)),
                  pl.BlockSpec((1, FOX_HEAD_DIM), lambda bi, i: (0, 0)),
                  pl.BlockSpec((tq, tq), lambda bi, i: (0, 0))],
        out_specs=[pl.BlockSpec((None, tq, FOX_WIDTH), lambda bi, i: (bi, i, 0)),
                   pl.BlockSpec((None, tq, FOX_WIDTH), lambda bi, i: (bi, i, 0)),
                   pl.BlockSpec((None, tq, FOX_WIDTH), lambda bi, i: (bi, i, 0)),
                   pl.BlockSpec((None, FOX_HEADS, tq), lambda bi, i: (bi, 0, i))],
        out_shape=[act, act, act, jax.ShapeDtypeStruct((b, FOX_HEADS, s), F32)],
        scratch_shapes=[pltpu.VMEM((FOX_HEADS, tq), F32)],
        compiler_params=pltpu.CompilerParams(
            dimension_semantics=("parallel", "arbitrary"),
            vmem_limit_bytes=_vmem_limit(blk, temp_bytes=_nbytes((tq, qkv_w), F32))),
        name="fox_prep",
    )(pf3, h3, wf_t, f_bias.reshape(FOX_HEADS, 1), q_gain.reshape(1, -1), k_gain.reshape(1, -1), tri)


def _fox_attn_kernel(q_ref, k_ref, v_ref, c_ref, gate_ref, o_ref, m_ref, l_ref, acc_ref, s_ref, p_ref, *, t):
    qi = pl.program_id(2)
    m_ref[...] = jnp.full_like(m_ref, -jnp.inf)
    l_ref[...] = jnp.zeros_like(l_ref)
    acc_ref[...] = jnp.zeros_like(acc_ref)
    p_ref[1] = jnp.zeros((t, t), BF16)
    q = q_ref[...]
    c_base = c_ref[pl.ds(qi, 1), :][:, 0:1]

    def rows(ref, blk):
        return ref[pl.ds(pl.multiple_of(blk * t, t), t), :]

    def scores(ki):
        s = lax.dot_general(q, rows(k_ref, ki), (((1,), (1,)), ((), ())), preferred_element_type=F32)
        return s + (c_base - c_ref[pl.ds(ki, 1), :])

    def weighted_values(slot, blk):
        return jnp.dot(p_ref[slot], rows(v_ref, blk), preferred_element_type=F32)

    def step(ki, masked):
        s = s_ref[ki % 2]
        pv_prev = weighted_values((ki + 1) % 2, jnp.maximum(ki - 1, 0))
        if not masked:
            s_ref[(ki + 1) % 2] = scores(ki + 1)
        else:
            row = lax.broadcasted_iota(jnp.int32, (t, t), 0)
            col = lax.broadcasted_iota(jnp.int32, (t, t), 1)
            s = jnp.where(col <= row, s, -jnp.inf)
        m_prev = m_ref[...]
        m_new = jnp.maximum(m_prev, jnp.max(s, axis=1, keepdims=True))
        alpha = jnp.exp2(m_prev - m_new)
        p = jnp.exp2(s - jnp.tile(m_new, (1, t // LANES)))
        l_ref[...] = alpha * l_ref[...] + jnp.sum(p, axis=1, keepdims=True)
        acc_ref[...] = alpha * (acc_ref[...] + pv_prev)
        m_ref[...] = m_new
        p_ref[ki % 2] = p.astype(BF16)

    s_ref[0] = scores(0)

    def body(ki, carry):
        step(ki, False)
        return carry

    lax.fori_loop(0, qi, body, 0)
    step(qi, True)
    o = (acc_ref[...] + weighted_values(qi % 2, qi)) / l_ref[...]
    o_ref[...] = (o * jax.nn.sigmoid(gate_ref[...])).astype(o_ref.dtype)


def fox_attention(qn, kn, vb, c, pf3, t=512):
    b, s, _ = qn.shape
    t = min(t, s)
    nt = s // t
    c4 = c.reshape(b, FOX_HEADS, nt, t)
    gate_blk0 = 3 * FOX_WIDTH // FOX_HEAD_DIM
    blk = (2 * _nbytes((t, LANES), BF16) + 2 * _nbytes((s, LANES), BF16) + _nbytes((nt, t), F32)
           + _nbytes((t, LANES), F32))
    return pl.pallas_call(
        functools.partial(_fox_attn_kernel, t=t),
        grid=(b, FOX_HEADS, nt),
        in_specs=[pl.BlockSpec((None, t, FOX_HEAD_DIM), lambda bi, h, qi: (bi, qi, h)),
                  pl.BlockSpec((None, s, FOX_HEAD_DIM), lambda bi, h, qi: (bi, 0, h)),
                  pl.BlockSpec((None, s, FOX_HEAD_DIM), lambda bi, h, qi: (bi, 0, h)),
                  pl.BlockSpec((None, None, nt, t), lambda bi, h, qi: (bi, h, 0, 0)),
                  pl.BlockSpec((None, t, FOX_HEAD_DIM), lambda bi, h, qi: (bi, qi, gate_blk0 + h))],
        out_specs=pl.BlockSpec((None, t, FOX_HEAD_DIM), lambda bi, h, qi: (bi, qi, h)),
        out_shape=jax.ShapeDtypeStruct((b, s, FOX_WIDTH), BF16),
        scratch_shapes=[pltpu.VMEM((t, LANES), F32), pltpu.VMEM((t, LANES), F32),
                        pltpu.VMEM((t, FOX_HEAD_DIM), F32),
                        pltpu.VMEM((2, t, t), F32), pltpu.VMEM((2, t, t), BF16)],
        compiler_params=pltpu.CompilerParams(
            dimension_semantics=("parallel", "parallel", "arbitrary"),
            vmem_limit_bytes=_vmem_limit(blk, scratch_bytes=3 * _nbytes((t, LANES), F32) + 3 * _nbytes((t, t), F32),
                                         temp_bytes=6 * _nbytes((t, t), F32))),
        name="fox_attention",
    )(qn, kn, vb, c4, pf3)


def _to_column_layout(x, axis=-1):
    x = jnp.moveaxis(x, axis, -1)
    y = x.reshape(x.shape[:-1] + (RWKV_HEADS, RWKV_HEAD_DIM)).swapaxes(-1, -2).reshape(x.shape)
    return jnp.moveaxis(y, -1, axis)


def _to_value_layout(x, axis=-1):
    x = jnp.moveaxis(x, axis, -1)
    lead = x.ndim - 1
    y = x.reshape(x.shape[:-1] + (RWKV_HEADS, 2, LANE_GROUPS, SUBLANES))
    y = y.transpose(tuple(range(lead)) + (lead + 1, lead + 3, lead + 2, lead)).reshape(x.shape)
    return jnp.moveaxis(y, -1, axis)


def _group_allreduce(x, axis):
    x = x + pltpu.roll(x, 2 * RWKV_HEADS, axis=axis)
    return x + pltpu.roll(x, RWKV_HEADS, axis=axis)


def _head_sum(x):
    s = x[:, 0:LANES]
    for row in range(1, PACK_ROWS):
        s = s + x[:, row * LANES:(row + 1) * LANES]
    return _group_allreduce(s, 1)


def _rwkv_prep_kernel(p_ref, pprev_ref, mu_ref, w0_ref, w2_ref, a0_ref, a2_ref, g2_ref, kk_ref, ka_ref,
                      rk_ref, r_o, w_o, k_o, v_o, a_o, b_o, g_o, bonus_o, *, tiles_per_seq):
    first = (pl.program_id(0) % tiles_per_seq) == 0

    def shifted(lo, hi):
        p = p_ref[:, lo:hi]
        prev_row = jnp.where(first, 0.0, pprev_ref[SUBLANES - 1:SUBLANES, lo:hi])
        row = lax.broadcasted_iota(jnp.int32, p.shape, 0)
        prev = jnp.where(row == 0, prev_row, pltpu.roll(p, 1, axis=0))
        return p + (prev - p) * mu_ref[:, lo:hi]

    def over_rows(s):
        return jnp.tile(s, (1, PACK_ROWS))

    w_lo = shifted(WLO_OFF, WLO_OFF + DECAY_LORA)
    z = w0_ref[...] + jnp.dot(jnp.tanh(w_lo), w2_ref[...], precision=HIGHEST, preferred_element_type=F32)
    softplus_neg = jnp.maximum(-z, 0.0) + jnp.log1p(jnp.exp(-jnp.abs(z)))
    w_o[...] = jnp.exp(-jnp.exp(-softplus_neg - 0.5))

    a_lo = shifted(ALO_OFF, ALO_OFF + AAA_LORA)
    a_lr = jax.nn.sigmoid(a0_ref[...] + jnp.dot(a_lo, a2_ref[...], precision=HIGHEST,
                                                preferred_element_type=F32))

    g_lo = shifted(GLO_OFF, GLO_OFF + GATE_PAD)
    g_o[...] = jnp.dot(jax.nn.sigmoid(g_lo).astype(BF16), g2_ref[...], preferred_element_type=F32)

    k = shifted(K_OFF, K_OFF + RWKV_WIDTH)
    kk = k * kk_ref[...]
    kk = kk * over_rows(lax.rsqrt(jnp.maximum(_head_sum(kk * kk), 1e-24)))
    a_o[...] = -kk
    b_o[...] = kk * a_lr
    k = k * (1.0 + (a_lr - 1.0) * ka_ref[...])
    k_o[...] = k

    r = shifted(R_OFF, R_OFF + RWKV_WIDTH)
    r_o[...] = r
    v = shifted(V_OFF, V_OFF + RWKV_WIDTH)
    v_o[...] = v
    bonus_o[...] = over_rows(_head_sum(r * k * rk_ref[...])) * v


def rwkv_prep(pr, seq, mu, w0, w2, a0, a2, g2p, k_k, k_a, r_k, tq=128):
    t_total = pr.shape[0]
    width = RWKV_READ
    tq = min(tq, seq)
    row = lambda x: x.reshape(1, -1)
    full = lambda shape: pl.BlockSpec(shape, lambda i: (0, 0))
    out = jax.ShapeDtypeStruct((t_total, RWKV_WIDTH), F32)
    out_spec = pl.BlockSpec((tq, RWKV_WIDTH), lambda i: (i, 0))
    sub_per_tile = tq // SUBLANES
    blk = (_nbytes((tq, width), F32) + _nbytes((SUBLANES, width), F32) + 8 * _nbytes((tq, RWKV_WIDTH), F32)
           + _nbytes((DECAY_LORA + AAA_LORA, RWKV_WIDTH), F32) + _nbytes((GATE_PAD, RWKV_WIDTH), BF16))
    return pl.pallas_call(
        functools.partial(_rwkv_prep_kernel, tiles_per_seq=seq // tq),
        grid=(t_total // tq,),
        in_specs=[pl.BlockSpec((tq, width), lambda i: (i, 0)),
                  pl.BlockSpec((SUBLANES, width), lambda i: (jnp.maximum(i * sub_per_tile - 1, 0), 0)),
                  full((1, width)), full((1, RWKV_WIDTH)), full((DECAY_LORA, RWKV_WIDTH)),
                  full((1, RWKV_WIDTH)), full((AAA_LORA, RWKV_WIDTH)), full((GATE_PAD, RWKV_WIDTH)),
                  full((1, RWKV_WIDTH)), full((1, RWKV_WIDTH)), full((1, RWKV_WIDTH))],
        out_specs=[out_spec] * 8,
        out_shape=[out] * 8,
        compiler_params=pltpu.CompilerParams(
            dimension_semantics=("parallel",),
            vmem_limit_bytes=_vmem_limit(blk, temp_bytes=12 * _nbytes((tq, RWKV_WIDTH), F32))),
        name="rwkv_prep",
    )(pr, pr, row(mu), row(w0), w2, row(a0), a2, g2p, row(k_k), row(k_a), row(r_k))


V_TILES = RWKV_HEAD_DIM // SUBLANES


def _rwkv_scan_kernel(r_ref, w_ref, k_ref, v_ref, a_ref, b_ref, bonus_ref, g_ref, gain_ref, bias_ref, o_ref,
                      state_ref, sa_ref, vrep_ref, y_ref, *, tc, nb):
    @pl.when(pl.program_id(0) == 0)
    def _():
        state_ref[...] = jnp.zeros_like(state_ref)

    lane_group = lax.broadcasted_iota(jnp.int32, (1, 1, LANES), 2) // RWKV_HEADS

    def by_lane_group(pick):
        out = pick(LANE_GROUPS - 1)
        for grp in range(LANE_GROUPS - 2, -1, -1):
            out = jnp.where(lane_group == grp, pick(grp), out)
        return out

    for b in range(nb):
        for gp in range(2):
            x = v_ref[b, :, gp * SUBLANES:(gp + 1) * SUBLANES, :]
            rolled = [x] + [pltpu.roll(x, j * RWKV_HEADS, axis=2) for j in range(1, LANE_GROUPS)]
            for q in range(LANE_GROUPS):
                tile = gp * LANE_GROUPS + q
                vrep_ref[b, :, tile * SUBLANES:(tile + 1) * SUBLANES, :] = by_lane_group(
                    lambda grp: rolled[(grp - q) % LANE_GROUPS])

    def bcast(ref, b, t, row):
        return jnp.broadcast_to(ref[b, t, row:row + 1, :], (SUBLANES, LANES))

    def tile(g):
        return pl.ds(g * SUBLANES, SUBLANES)

    def step(t, carry):
        for b in range(nb):
            sa = [None] * V_TILES
            for row in range(PACK_ROWS):
                ab = bcast(a_ref, b, t, row)
                for g in range(V_TILES):
                    term = state_ref[b, row, tile(g), :] * ab
                    sa[g] = term if row == 0 else sa[g] + term
            for g in range(V_TILES):
                sa_ref[b, tile(g), :] = _group_allreduce(sa[g], 1)
            y = [None] * V_TILES
            for row in range(PACK_ROWS):
                wb, bb = bcast(w_ref, b, t, row), bcast(b_ref, b, t, row)
                kb, rb = bcast(k_ref, b, t, row), bcast(r_ref, b, t, row)
                for g in range(V_TILES):
                    s = (state_ref[b, row, tile(g), :] * wb + sa_ref[b, tile(g), :] * bb
                         + vrep_ref[b, t, tile(g), :] * kb)
                    state_ref[b, row, tile(g), :] = s
                    y[g] = s * rb if row == 0 else y[g] + s * rb
            for g in range(V_TILES):
                y_ref[b, t, tile(g), :] = y[g]
        return carry

    lax.fori_loop(0, tc, step, 0)

    inv_n = 1.0 / RWKV_HEAD_DIM
    for b in range(nb):
        halves = []
        for gp in range(2):
            tiles = [y_ref[b, :, (gp * LANE_GROUPS + q) * SUBLANES:(gp * LANE_GROUPS + q + 1) * SUBLANES, :]
                     for q in range(LANE_GROUPS)]
            packed = by_lane_group(lambda grp: tiles[grp])
            for j in range(1, LANE_GROUPS):
                m_j = by_lane_group(lambda grp: tiles[(grp + j) % LANE_GROUPS])
                packed = packed + pltpu.roll(m_j, j * RWKV_HEADS, axis=2)
            halves.append(packed)
        y = jnp.concatenate(halves, axis=1)
        mean = _group_allreduce(jnp.sum(y, axis=1, keepdims=True), 2) * inv_n
        yc = y - mean
        var = _group_allreduce(jnp.sum(yc * yc, axis=1, keepdims=True), 2) * inv_n
        yn = yc * lax.rsqrt(var + GN_EPS)
        out = (yn * gain_ref[...] + bias_ref[...] + bonus_ref[b]) * g_ref[b]
        o_ref[b] = out.astype(o_ref.dtype)


def rwkv_scan(r, w, k, v, a, b, bonus, g, gn_gain, gn_bias, batch, tc=32):
    t_total = r.shape[0]
    s = t_total // batch
    tc = min(tc, s)
    packed = lambda x: x.reshape(batch, s, PACK_ROWS, LANES)
    spec = pl.BlockSpec((batch, tc, PACK_ROWS, LANES), lambda i: (0, i, 0, 0))
    affine = pl.BlockSpec((PACK_ROWS, LANES), lambda i: (0, 0))
    blk = 8 * _nbytes((batch, tc, PACK_ROWS, LANES), F32) + _nbytes((batch, tc, PACK_ROWS, LANES), BF16)
    slab = _nbytes((batch, tc, RWKV_HEAD_DIM, LANES), F32)
    state_bytes = _nbytes((batch, PACK_ROWS + 1, RWKV_HEAD_DIM, LANES), F32)
    out = pl.pallas_call(
        functools.partial(_rwkv_scan_kernel, tc=tc, nb=batch),
        grid=(s // tc,),
        in_specs=[spec] * 8 + [affine, affine],
        out_specs=spec,
        out_shape=jax.ShapeDtypeStruct((batch, s, PACK_ROWS, LANES), BF16),
        scratch_shapes=[pltpu.VMEM((batch, PACK_ROWS, RWKV_HEAD_DIM, LANES), F32),
                        pltpu.VMEM((batch, RWKV_HEAD_DIM, LANES), F32),
                        pltpu.VMEM((batch, tc, RWKV_HEAD_DIM, LANES), F32),
                        pltpu.VMEM((batch, tc, RWKV_HEAD_DIM, LANES), F32)],
        compiler_params=pltpu.CompilerParams(
            dimension_semantics=("arbitrary",),
            vmem_limit_bytes=_vmem_limit(blk, scratch_bytes=2 * slab + state_bytes, temp_bytes=6 * slab)),
        name="rwkv_scan",
    )(packed(r), packed(w), packed(k), packed(v), packed(a), packed(b), packed(bonus), packed(g),
      gn_gain.reshape(PACK_ROWS, LANES), gn_bias.reshape(PACK_ROWS, LANES))
    return out.reshape(t_total, RWKV_WIDTH)


def _mem_attn_kernel(q_ref, k_ref, v_ref, qg_ref, kg_ref, o_ref):
    scale = MEM_HEAD_DIM ** -0.5
    for h in range(MEM_HEADS):
        cols = slice(h * MEM_HEAD_DIM, (h + 1) * MEM_HEAD_DIM)
        q = q_ref[:, cols]
        q = q * lax.rsqrt(jnp.mean(q * q, axis=-1, keepdims=True) + NORM_EPS) * qg_ref[...]
        k = k_ref[:, cols]
        k = k * lax.rsqrt(jnp.mean(k * k, axis=-1, keepdims=True) + NORM_EPS) * kg_ref[...]
        s = lax.dot_general(q.astype(BF16), k.astype(BF16), (((1,), (1,)), ((), ())),
                            preferred_element_type=F32) * scale
        p = jnp.exp(s - jnp.max(s, axis=-1, keepdims=True))
        p = p / jnp.sum(p, axis=-1, keepdims=True)
        o = jnp.dot(p.astype(BF16), v_ref[:, cols].astype(BF16), preferred_element_type=F32)
        o_ref[:, cols] = o.astype(o_ref.dtype)


def mem_attention(q3, k3, v3, q_gain, k_gain, tq=512):
    b, s, _ = q3.shape
    m = k3.shape[1]
    tq = min(tq, s)
    blk = _nbytes((tq, MEM_WIDTH), F32) + 2 * _nbytes((m, MEM_WIDTH), F32) + _nbytes((tq, MEM_WIDTH), BF16)
    return pl.pallas_call(
        _mem_attn_kernel,
        grid=(b, s // tq),
        in_specs=[pl.BlockSpec((None, tq, MEM_WIDTH), lambda bi, i: (bi, i, 0)),
                  pl.BlockSpec((None, m, MEM_WIDTH), lambda bi, i: (bi, 0, 0)),
                  pl.BlockSpec((None, m, MEM_WIDTH), lambda bi, i: (bi, 0, 0)),
                  pl.BlockSpec((1, MEM_HEAD_DIM), lambda bi, i: (0, 0)),
                  pl.BlockSpec((1, MEM_HEAD_DIM), lambda bi, i: (0, 0))],
        out_specs=pl.BlockSpec((None, tq, MEM_WIDTH), lambda bi, i: (bi, i, 0)),
        out_shape=jax.ShapeDtypeStruct((b, s, MEM_WIDTH), BF16),
        compiler_params=pltpu.CompilerParams(
            dimension_semantics=("parallel", "parallel"),
            vmem_limit_bytes=_vmem_limit(blk, temp_bytes=8 * _nbytes((tq, m), F32))),
        name="mem_attention",
    )(q3, k3, v3, q_gain.reshape(1, -1), k_gain.reshape(1, -1))


def _rwkv_projection_weight(w_in, layer):
    def cols(lo, hi):
        return w_in[layer, :, FOX_IN + lo:FOX_IN + hi].astype(BF16)

    parts = [_to_column_layout(cols(R_OFF, WLO_OFF)), cols(WLO_OFF, K_OFF), _to_column_layout(cols(K_OFF, V_OFF)),
             _to_value_layout(cols(V_OFF, ALO_OFF)), cols(ALO_OFF, RWKV_IN),
             jnp.zeros((w_in.shape[1], RWKV_PAD - RWKV_IN), BF16)]
    return jnp.concatenate(parts, axis=1)


def _rwkv_shift_mix(mu):
    parts = [_to_column_layout(mu[R_OFF:WLO_OFF]), mu[WLO_OFF:K_OFF], _to_column_layout(mu[K_OFF:V_OFF]),
             _to_value_layout(mu[V_OFF:ALO_OFF]), mu[ALO_OFF:], jnp.zeros((RWKV_READ - RWKV_IN,), mu.dtype)]
    return jnp.concatenate(parts)


def _layer(x, mem2, b, s, layer, p, w):
    t_total = b * s
    h = rmsnorm_bf16(x, p["norm_mix"])
    pf = matmul(h, w["w_in"], layer=layer, n=FOX_QKVG, tm=2048, single_buffer_a=True)
    pr = matmul(h, _rwkv_projection_weight(w["w_in"], layer), tm=2048, single_buffer_a=True)

    pf3 = pf.reshape(b, s, FOX_QKVG)
    wf_t = w["w_in"][layer, :, FOX_QKVG:FOX_IN].T.astype(BF16)
    qn, kn, vb, c = fox_prep(pf3, h.reshape(b, s, D_MODEL), wf_t, p["fox_f_bias"],
                             p["fox_q_gain"], p["fox_k_gain"])
    y_fox = fox_attention(qn, kn, vb, c, pf3).reshape(t_total, FOX_WIDTH)

    g2p = _to_value_layout(jnp.pad(p["rwkv_g2"], ((0, GATE_PAD - GATE_LORA), (0, 0)))).astype(BF16)
    r, wd, k, v, a, bb, g, bonus = rwkv_prep(
        pr, s, _rwkv_shift_mix(p["rwkv_mu"]), _to_column_layout(p["rwkv_w0"]), _to_column_layout(p["rwkv_w2"]),
        _to_column_layout(p["rwkv_a0"]), _to_column_layout(p["rwkv_a2"]), g2p, _to_column_layout(p["rwkv_k_k"]),
        _to_column_layout(p["rwkv_k_a"]), _to_column_layout(p["rwkv_r_k"].reshape(-1)))
    y_rwkv = rwkv_scan(r, wd, k, v, a, bb, bonus, g, _to_value_layout(p["rwkv_gn_gain"]),
                       _to_value_layout(p["rwkv_gn_bias"]), b)

    w_rwkv_rows = _to_value_layout(w["w_out"][layer, FOX_WIDTH:, :].astype(BF16), axis=0)
    x = out_proj(y_fox, y_rwkv, w["w_out"], layer, w_rwkv_rows, x)

    h = rmsnorm_bf16(x, p["norm_mem_q"])
    m = rmsnorm_bf16(mem2, p["norm_mem_kv"])
    q = matmul(h, w["mem_w_q"], layer=layer)
    km = matmul(m, w["mem_w_k"], layer=layer)
    vm = matmul(m, w["mem_w_v"], layer=layer)
    n_mem = mem2.shape[0] // b
    o = mem_attention(q.reshape(b, s, MEM_WIDTH), km.reshape(b, n_mem, MEM_WIDTH),
                      vm.reshape(b, n_mem, MEM_WIDTH), p["mem_q_gain"], p["mem_k_gain"])
    x = matmul(o.reshape(t_total, MEM_WIDTH), w["mem_w_o"], layer=layer, epilogue="residual", residual=x,
               tn=1024)

    h = rmsnorm_bf16(x, p["norm_mlp"])
    u = matmul(h, w["w_up"], layer=layer, epilogue="relu2", out_dtype=BF16, tm=2048, single_buffer_a=True)
    x = matmul(u, w["w_down"], layer=layer, epilogue="residual", residual=x, tm=1024, tn=1024, tk=2048)
    return x


_PARAM_NAMES = ("norm_mix", "w_in", "fox_q_gain", "fox_k_gain", "fox_f_bias", "rwkv_mu", "rwkv_w0", "rwkv_w2",
                "rwkv_a0", "rwkv_a2", "rwkv_g2", "rwkv_k_k", "rwkv_k_a", "rwkv_r_k", "rwkv_gn_gain",
                "rwkv_gn_bias", "w_out", "norm_mem_q", "norm_mem_kv", "mem_w_q", "mem_w_k", "mem_w_v",
                "mem_q_gain", "mem_k_gain", "mem_w_o", "norm_mlp", "w_up", "w_down")
_STACKED_WEIGHTS = ("w_in", "w_out", "mem_w_q", "mem_w_k", "mem_w_v", "mem_w_o", "w_up", "w_down")


def kernel(x, mem, norm_mix, w_in, fox_q_gain, fox_k_gain, fox_f_bias, rwkv_mu, rwkv_w0, rwkv_w2, rwkv_a0, rwkv_a2, rwkv_g2, rwkv_k_k, rwkv_k_a, rwkv_r_k, rwkv_gn_gain, rwkv_gn_bias, w_out, norm_mem_q, norm_mem_kv, mem_w_q, mem_w_k, mem_w_v, mem_q_gain, mem_k_gain, mem_w_o, norm_mlp, w_up, w_down):
    params = dict(zip(_PARAM_NAMES, (norm_mix, w_in, fox_q_gain, fox_k_gain, fox_f_bias, rwkv_mu, rwkv_w0,
                                     rwkv_w2, rwkv_a0, rwkv_a2, rwkv_g2, rwkv_k_k, rwkv_k_a, rwkv_r_k,
                                     rwkv_gn_gain, rwkv_gn_bias, w_out, norm_mem_q, norm_mem_kv, mem_w_q,
                                     mem_w_k, mem_w_v, mem_q_gain, mem_k_gain, mem_w_o, norm_mlp, w_up, w_down)))
    b, s, d = x.shape
    assert d == D_MODEL, x.shape
    depth = w_in.shape[0]
    stacked = {name: params[name] for name in _STACKED_WEIGHTS}
    x2 = x.reshape(b * s, d)
    mem2 = mem.reshape(-1, d)
    for layer in range(depth):
        small = {name: value[layer] for name, value in params.items() if name not in _STACKED_WEIGHTS}
        x2 = _layer(x2, mem2, b, s, layer, small, stacked)
    return x2.reshape(b, s, d)
```

```python
import functools

import jax
import jax.numpy as jnp
from jax import lax
from jax.experimental import pallas as pl
from jax.experimental.pallas import tpu as pltpu

D_MODEL = 4096
FOX_WIDTH = 2048
FOX_HEAD_DIM = 128
FOX_HEADS = 16
RWKV_WIDTH = 2048
RWKV_HEAD_DIM = 64
RWKV_HEADS = 32
DECAY_LORA = 128
AAA_LORA = 128
GATE_LORA = 480
MEM_HEADS = 4
MEM_HEAD_DIM = 128
MEM_WIDTH = MEM_HEADS * MEM_HEAD_DIM
NORM_EPS = 1e-6
GN_EPS = 64e-5
FOX_QKVG = 4 * FOX_WIDTH
FOX_IN = FOX_QKVG + FOX_HEADS
RWKV_IN = 3 * RWKV_WIDTH + DECAY_LORA + AAA_LORA + GATE_LORA
R_OFF, WLO_OFF, K_OFF, V_OFF, ALO_OFF, GLO_OFF = 0, 2048, 2176, 4224, 6272, 6400

LANES = 128
SUBLANES = 8
VMEM_BYTES_V7X = 64 * 1024 * 1024
RWKV_READ = 6912
RWKV_PAD = 7168
GATE_PAD = RWKV_READ - GLO_OFF

LANE_GROUPS = LANES // RWKV_HEADS
PACK_ROWS = RWKV_WIDTH // LANES

F32 = jnp.float32
BF16 = jnp.bfloat16
HIGHEST = lax.Precision.HIGHEST
LOG2_E = 1.4426950408889634


def _vmem_limit(block_bytes, scratch_bytes=0, temp_bytes=0):
    need = 2 * block_bytes + scratch_bytes + temp_bytes + (4 << 20)
    return int(min(need, VMEM_BYTES_V7X - (6 << 20)))


def _nbytes(shape, dtype):
    n = 1
    for s in shape:
        n *= s
    return n * jnp.dtype(dtype).itemsize


def _rmsnorm_kernel(x_ref, g_ref, o_ref):
    x = x_ref[...]
    ms = jnp.mean(x * x, axis=-1, keepdims=True)
    o_ref[...] = (x * lax.rsqrt(ms + NORM_EPS) * g_ref[...]).astype(o_ref.dtype)


def rmsnorm_bf16(x, gain, tm=256):
    m, d = x.shape
    tm = min(tm, m)
    return pl.pallas_call(
        _rmsnorm_kernel,
        grid=(m // tm,),
        in_specs=[pl.BlockSpec((tm, d), lambda i: (i, 0)),
                  pl.BlockSpec((1, d), lambda i: (0, 0))],
        out_specs=pl.BlockSpec((tm, d), lambda i: (i, 0)),
        out_shape=jax.ShapeDtypeStruct((m, d), BF16),
        compiler_params=pltpu.CompilerParams(
            dimension_semantics=("parallel",),
            vmem_limit_bytes=_vmem_limit(_nbytes((tm, d), F32) + _nbytes((tm, d), BF16),
                                         temp_bytes=2 * _nbytes((tm, d), F32))),
        name="rmsnorm",
    )(x, gain.reshape(1, d))


def _matmul_kernel(*refs, epilogue, nk):
    if epilogue == "residual":
        a_ref, w_ref, r_ref, o_ref = refs
    else:
        a_ref, w_ref, o_ref = refs

    def product():
        return jnp.dot(a_ref[...], w_ref[...].astype(BF16), preferred_element_type=F32)

    if nk == 1:
        d = product()
        if epilogue == "relu2":
            r = jnp.maximum(d, 0.0)
            o_ref[...] = (r * r).astype(o_ref.dtype)
        elif epilogue == "residual":
            o_ref[...] = r_ref[...] + d
        else:
            o_ref[...] = d.astype(o_ref.dtype)
    else:
        k = pl.program_id(2)

        @pl.when(k == 0)
        def _():
            o_ref[...] = r_ref[...] + product() if epilogue == "residual" else product()

        @pl.when(k > 0)
        def _():
            o_ref[...] += product()


def matmul(a, w, *, layer=None, n=None, epilogue="none", residual=None, out_dtype=F32,
           tm=1024, tn=512, tk=None, single_buffer_a=False):
    m, kdim = a.shape
    n = w.shape[-1] if n is None else n
    tk = kdim if tk is None else tk
    tm, tn = min(tm, m), min(tn, n)
    assert m % tm == 0 and n % tn == 0 and kdim % tk == 0, (a.shape, w.shape, tm, tn, tk)
    nk = kdim // tk
    assert nk == 1 or (out_dtype == F32 and epilogue != "relu2")
    if layer is None:
        w_spec = pl.BlockSpec((tk, tn), lambda i, j, k: (k, j))
    else:
        w_spec = pl.BlockSpec((None, tk, tn), lambda i, j, k: (layer, k, j))
    a_mode = pl.Buffered(1) if single_buffer_a else None
    in_specs = [pl.BlockSpec((tm, tk), lambda i, j, k: (i, k), pipeline_mode=a_mode), w_spec]
    args = [a, w]
    a_bytes = _nbytes((tm, tk), BF16)
    blk = (a_bytes // 2 if single_buffer_a else a_bytes) + _nbytes((tk, tn), w.dtype) + _nbytes((tm, tn), out_dtype)
    if epilogue == "residual":
        in_specs.append(pl.BlockSpec((tm, tn), lambda i, j, k: (i, j)))
        args.append(residual)
        blk += _nbytes((tm, tn), F32)
    return pl.pallas_call(
        functools.partial(_matmul_kernel, epilogue=epilogue, nk=nk),
        grid=(m // tm, n // tn, nk),
        in_specs=in_specs,
        out_specs=pl.BlockSpec((tm, tn), lambda i, j, k: (i, j)),
        out_shape=jax.ShapeDtypeStruct((m, n), out_dtype),
        compiler_params=pltpu.CompilerParams(
            dimension_semantics=("parallel", "parallel", "arbitrary"),
            vmem_limit_bytes=_vmem_limit(blk, temp_bytes=_nbytes((tk, tn), BF16) + 2 * _nbytes((tm, tn), F32))),
        name="matmul_" + epilogue,
    )(*args)


def _out_proj_kernel(a1_ref, a2_ref, w1_ref, w2_ref, r_ref, o_ref):
    d = jnp.dot(a1_ref[...], w1_ref[...].astype(BF16), preferred_element_type=F32)
    d = d + jnp.dot(a2_ref[...], w2_ref[...].astype(BF16), preferred_element_type=F32)
    o_ref[...] = r_ref[...] + d


def out_proj(y_fox, y_rwkv, w_out, layer, w_rwkv_rows, x, tm=1024, tn=512):
    m, half = y_fox.shape
    n = x.shape[1]
    tm = min(tm, m)
    assert m % tm == 0 and n % tn == 0, (m, n, tm, tn)
    blk = 2 * _nbytes((tm, half), BF16) + 2 * _nbytes((half, tn), F32) + 2 * _nbytes((tm, tn), F32)
    return pl.pallas_call(
        _out_proj_kernel,
        grid=(m // tm, n // tn),
        in_specs=[pl.BlockSpec((tm, half), lambda i, j: (i, 0)),
                  pl.BlockSpec((tm, half), lambda i, j: (i, 0)),
                  pl.BlockSpec((None, half, tn), lambda i, j: (layer, 0, j)),
                  pl.BlockSpec((half, tn), lambda i, j: (0, j)),
                  pl.BlockSpec((tm, tn), lambda i, j: (i, j))],
        out_specs=pl.BlockSpec((tm, tn), lambda i, j: (i, j)),
        out_shape=jax.ShapeDtypeStruct((m, n), F32),
        compiler_params=pltpu.CompilerParams(
            dimension_semantics=("parallel", "parallel"),
            vmem_limit_bytes=_vmem_limit(blk, temp_bytes=2 * _nbytes((half, tn), BF16) + 2 * _nbytes((tm, tn), F32))),
        name="out_proj",
    )(y_fox, y_rwkv, w_out, w_rwkv_rows, x)


def _fox_prep_kernel(p_ref, h_ref, wf_ref, fb_ref, qg_ref, kg_ref, tri_ref,
                     q_o, k_o, v_o, c_o, carry_ref, *, tq):
    @pl.when(pl.program_id(1) == 0)
    def _():
        carry_ref[...] = jnp.zeros_like(carry_ref)

    scale = FOX_HEAD_DIM ** -0.5 * LOG2_E
    for h in range(FOX_HEADS):
        lo = h * FOX_HEAD_DIM
        q = p_ref[:, lo:lo + FOX_HEAD_DIM]
        ms = jnp.mean(q * q, axis=-1, keepdims=True)
        q_o[:, lo:lo + FOX_HEAD_DIM] = (q * lax.rsqrt(ms + NORM_EPS) * qg_ref[...] * scale).astype(BF16)
        k = p_ref[:, FOX_WIDTH + lo:FOX_WIDTH + lo + FOX_HEAD_DIM]
        ms = jnp.mean(k * k, axis=-1, keepdims=True)
        k_o[:, lo:lo + FOX_HEAD_DIM] = (k * lax.rsqrt(ms + NORM_EPS) * kg_ref[...]).astype(BF16)
    v_o[...] = p_ref[:, 2 * FOX_WIDTH:3 * FOX_WIDTH].astype(BF16)

    f_logit = jnp.dot(h_ref[...], wf_ref[...].astype(BF16), preferred_element_type=F32)
    log_f = jax.nn.log_sigmoid(f_logit + fb_ref[...])
    c = jnp.dot(tri_ref[...], log_f, precision=HIGHEST, preferred_element_type=F32) + carry_ref[0:1, :]
    c_o[...] = c * LOG2_E
    carry_ref[...] = jnp.broadcast_to(c[tq - 1:tq, :], carry_ref.shape)


def fox_prep(pf3, h3, w_in, layer, f_bias, q_gain, k_gain, tq=512):
    b, s, _ = pf3.shape
    tq = min(tq, s)
    tri = (lax.broadcasted_iota(jnp.int32, (tq, tq), 0) >= lax.broadcasted_iota(jnp.int32, (tq, tq), 1)).astype(F32)
    qkv_w = 3 * FOX_WIDTH
    act = jax.ShapeDtypeStruct((b, s, FOX_WIDTH), BF16)
    fb = jnp.pad(f_bias, (0, LANES - FOX_HEADS)).reshape(1, LANES)
    blk = (_nbytes((tq, qkv_w), F32) + _nbytes((tq, D_MODEL), BF16) + _nbytes((D_MODEL, LANES), F32)
           + _nbytes((tq, tq), F32) + 3 * _nbytes((tq, FOX_WIDTH), BF16) + _nbytes((tq, LANES), F32))
    qn, kn, vb, c = pl.pallas_call(
        functools.partial(_fox_prep_kernel, tq=tq),
        grid=(b, s // tq),
        in_specs=[pl.BlockSpec((None, tq, qkv_w), lambda bi, i: (bi, i, 0)),
                  pl.BlockSpec((None, tq, D_MODEL), lambda bi, i: (bi, i, 0)),
                  pl.BlockSpec((None, D_MODEL, LANES), lambda bi, i: (layer, 0, FOX_QKVG // LANES)),
                  pl.BlockSpec((1, LANES), lambda bi, i: (0, 0)),
                  pl.BlockSpec((1, FOX_HEAD_DIM), lambda bi, i: (0, 0)),
                  pl.BlockSpec((1, FOX_HEAD_DIM), lambda bi, i: (0, 0)),
                  pl.BlockSpec((tq, tq), lambda bi, i: (0, 0))],
        out_specs=[pl.BlockSpec((None, tq, FOX_WIDTH), lambda bi, i: (bi, i, 0)),
                   pl.BlockSpec((None, tq, FOX_WIDTH), lambda bi, i: (bi, i, 0)),
                   pl.BlockSpec((None, tq, FOX_WIDTH), lambda bi, i: (bi, i, 0)),
                   pl.BlockSpec((None, tq, LANES), lambda bi, i: (bi, i, 0))],
        out_shape=[act, act, act, jax.ShapeDtypeStruct((b, s, LANES), F32)],
        scratch_shapes=[pltpu.VMEM((SUBLANES, LANES), F32)],
        compiler_params=pltpu.CompilerParams(
            dimension_semantics=("parallel", "arbitrary"),
            vmem_limit_bytes=_vmem_limit(blk, temp_bytes=_nbytes((tq, qkv_w), F32))),
        name="fox_prep",
    )(pf3, h3, w_in, fb, q_gain.reshape(1, -1), k_gain.reshape(1, -1), tri)
    return qn, kn, vb, c[:, :, :FOX_HEADS].transpose(0, 2, 1)


FOX_HEADS_PER_STEP = 1


def _fox_attn_kernel(q_ref, k_ref, v_ref, c_ref, gate_ref, o_ref, m_ref, l_ref, acc_ref, s_ref, p_ref, *, t):
    qi = pl.program_id(2)
    heads = range(FOX_HEADS_PER_STEP)
    m_ref[...] = jnp.full_like(m_ref, -jnp.inf)
    l_ref[...] = jnp.zeros_like(l_ref)
    acc_ref[...] = jnp.zeros_like(acc_ref)
    for hh in heads:
        p_ref[hh, 1] = jnp.zeros((t, t), BF16)

    def cols(hh):
        return slice(hh * FOX_HEAD_DIM, (hh + 1) * FOX_HEAD_DIM)

    def rows(ref, blk, hh):
        return ref[pl.ds(pl.multiple_of(blk * t, t), t), cols(hh)]

    c_base = [c_ref[hh, pl.ds(qi, 1), :][:, 0:1] for hh in heads]

    def scores(ki, hh):
        s = lax.dot_general(q_ref[:, cols(hh)], rows(k_ref, ki, hh), (((1,), (1,)), ((), ())),
                            preferred_element_type=F32)
        return s + (c_base[hh] - c_ref[hh, pl.ds(ki, 1), :])

    def weighted_values(slot, blk, hh):
        return jnp.dot(p_ref[hh, slot], rows(v_ref, blk, hh), preferred_element_type=F32)

    def step(ki, masked):
        for hh in heads:
            s = s_ref[hh, ki % 2]
            pv_prev = weighted_values((ki + 1) % 2, jnp.maximum(ki - 1, 0), hh)
            if not masked:
                s_ref[hh, (ki + 1) % 2] = scores(ki + 1, hh)
            else:
                row = lax.broadcasted_iota(jnp.int32, (t, t), 0)
                col = lax.broadcasted_iota(jnp.int32, (t, t), 1)
                s = jnp.where(col <= row, s, -jnp.inf)
            m_prev = m_ref[hh]
            m_new = jnp.maximum(m_prev, jnp.max(s, axis=1, keepdims=True))
            alpha = jnp.exp2(m_prev - m_new)
            p = jnp.exp2(s - jnp.tile(m_new, (1, t // LANES)))
            l_ref[hh] = alpha * l_ref[hh] + jnp.sum(p, axis=1, keepdims=True)
            acc_ref[hh] = alpha * (acc_ref[hh] + pv_prev)
            m_ref[hh] = m_new
            p_ref[hh, ki % 2] = p.astype(BF16)

    for hh in heads:
        s_ref[hh, 0] = scores(0, hh)

    def body(ki, carry):
        step(ki, False)
        return carry

    lax.fori_loop(0, qi, body, 0)
    step(qi, True)
    for hh in heads:
        o = (acc_ref[hh] + weighted_values(qi % 2, qi, hh)) / l_ref[hh]
        o_ref[:, cols(hh)] = (o * jax.nn.sigmoid(gate_ref[:, cols(hh)])).astype(o_ref.dtype)


def fox_attention(qn, kn, vb, c, pf3, t=512):
    b, s, _ = qn.shape
    t = min(t, s)
    nt = s // t
    hp = FOX_HEADS_PER_STEP
    width = hp * FOX_HEAD_DIM
    c4 = c.reshape(b, FOX_HEADS, nt, t)
    gate_blk0 = 3 * FOX_WIDTH // width
    blk = (2 * _nbytes((t, width), BF16) + 2 * _nbytes((s, width), BF16) + hp * _nbytes((nt, t), F32)
           + _nbytes((t, width), F32))
    scratch = hp * (3 * _nbytes((t, LANES), F32) + 3 * _nbytes((t, t), F32))
    return pl.pallas_call(
        functools.partial(_fox_attn_kernel, t=t),
        grid=(b, FOX_HEADS // hp, nt),
        in_specs=[pl.BlockSpec((None, t, width), lambda bi, h, qi: (bi, qi, h)),
                  pl.BlockSpec((None, s, width), lambda bi, h, qi: (bi, 0, h)),
                  pl.BlockSpec((None, s, width), lambda bi, h, qi: (bi, 0, h)),
                  pl.BlockSpec((None, hp, nt, t), lambda bi, h, qi: (bi, h, 0, 0)),
                  pl.BlockSpec((None, t, width), lambda bi, h, qi: (bi, qi, gate_blk0 + h))],
        out_specs=pl.BlockSpec((None, t, width), lambda bi, h, qi: (bi, qi, h)),
        out_shape=jax.ShapeDtypeStruct((b, s, FOX_WIDTH), BF16),
        scratch_shapes=[pltpu.VMEM((hp, t, LANES), F32), pltpu.VMEM((hp, t, LANES), F32),
                        pltpu.VMEM((hp, t, FOX_HEAD_DIM), F32),
                        pltpu.VMEM((hp, 2, t, t), F32), pltpu.VMEM((hp, 2, t, t), BF16)],
        compiler_params=pltpu.CompilerParams(
            dimension_semantics=("parallel", "parallel", "arbitrary"),
            vmem_limit_bytes=_vmem_limit(blk, scratch_bytes=scratch, temp_bytes=8 * _nbytes((t, t), F32))),
        name="fox_attention",
    )(qn, kn, vb, c4, pf3)


def _to_column_layout(x, axis=-1):
    x = jnp.moveaxis(x, axis, -1)
    y = x.reshape(x.shape[:-1] + (RWKV_HEADS, RWKV_HEAD_DIM)).swapaxes(-1, -2).reshape(x.shape)
    return jnp.moveaxis(y, -1, axis)


def _to_value_layout(x, axis=-1):
    x = jnp.moveaxis(x, axis, -1)
    lead = x.ndim - 1
    y = x.reshape(x.shape[:-1] + (RWKV_HEADS, 2, LANE_GROUPS, SUBLANES))
    y = y.transpose(tuple(range(lead)) + (lead + 1, lead + 3, lead + 2, lead)).reshape(x.shape)
    return jnp.moveaxis(y, -1, axis)


def _group_allreduce(x, axis):
    x = x + pltpu.roll(x, 2 * RWKV_HEADS, axis=axis)
    return x + pltpu.roll(x, RWKV_HEADS, axis=axis)


def _head_sum(x):
    s = x[:, 0:LANES]
    for row in range(1, PACK_ROWS):
        s = s + x[:, row * LANES:(row + 1) * LANES]
    return _group_allreduce(s, 1)


def _rwkv_prep_kernel(p_ref, pprev_ref, mu_ref, w0_ref, w2_ref, a0_ref, a2_ref, g2_ref, kk_ref, ka_ref,
                      rk_ref, r_o, w_o, k_o, v_o, a_o, b_o, g_o, bonus_o, *, tiles_per_seq):
    first = (pl.program_id(0) % tiles_per_seq) == 0

    def shifted(lo, hi):
        p = p_ref[:, lo:hi]
        prev_row = jnp.where(first, 0.0, pprev_ref[SUBLANES - 1:SUBLANES, lo:hi])
        row = lax.broadcasted_iota(jnp.int32, p.shape, 0)
        prev = jnp.where(row == 0, prev_row, pltpu.roll(p, 1, axis=0))
        return p + (prev - p) * mu_ref[:, lo:hi]

    def over_rows(s):
        return jnp.tile(s, (1, PACK_ROWS))

    w_lo = shifted(WLO_OFF, WLO_OFF + DECAY_LORA)
    z = w0_ref[...] + jnp.dot(jnp.tanh(w_lo), w2_ref[...], precision=HIGHEST, preferred_element_type=F32)
    softplus_neg = jnp.maximum(-z, 0.0) + jnp.log1p(jnp.exp(-jnp.abs(z)))
    w_o[...] = jnp.exp(-jnp.exp(-softplus_neg - 0.5))

    a_lo = shifted(ALO_OFF, ALO_OFF + AAA_LORA)
    a_lr = jax.nn.sigmoid(a0_ref[...] + jnp.dot(a_lo, a2_ref[...], precision=HIGHEST,
                                                preferred_element_type=F32))

    g_lo = shifted(GLO_OFF, GLO_OFF + GATE_PAD)
    g_o[...] = jnp.dot(jax.nn.sigmoid(g_lo).astype(BF16), g2_ref[...], preferred_element_type=F32)

    k = shifted(K_OFF, K_OFF + RWKV_WIDTH)
    kk = k * kk_ref[...]
    kk = kk * over_rows(lax.rsqrt(jnp.maximum(_head_sum(kk * kk), 1e-24)))
    a_o[...] = -kk
    b_o[...] = kk * a_lr
    k = k * (1.0 + (a_lr - 1.0) * ka_ref[...])
    k_o[...] = k

    r = shifted(R_OFF, R_OFF + RWKV_WIDTH)
    r_o[...] = r
    v = shifted(V_OFF, V_OFF + RWKV_WIDTH)
    v_o[...] = v
    bonus_o[...] = over_rows(_head_sum(r * k * rk_ref[...])) * v


def rwkv_prep(pr, seq, mu, w0, w2, a0, a2, g2p, k_k, k_a, r_k, tq=128):
    t_total = pr.shape[0]
    width = RWKV_READ
    tq = min(tq, seq)
    row = lambda x: x.reshape(1, -1)
    full = lambda shape: pl.BlockSpec(shape, lambda i: (0, 0))
    out = jax.ShapeDtypeStruct((t_total, RWKV_WIDTH), F32)
    out_spec = pl.BlockSpec((tq, RWKV_WIDTH), lambda i: (i, 0))
    sub_per_tile = tq // SUBLANES
    blk = (_nbytes((tq, width), F32) + _nbytes((SUBLANES, width), F32) + 8 * _nbytes((tq, RWKV_WIDTH), F32)
           + _nbytes((DECAY_LORA + AAA_LORA, RWKV_WIDTH), F32) + _nbytes((GATE_PAD, RWKV_WIDTH), BF16))
    return pl.pallas_call(
        functools.partial(_rwkv_prep_kernel, tiles_per_seq=seq // tq),
        grid=(t_total // tq,),
        in_specs=[pl.BlockSpec((tq, width), lambda i: (i, 0)),
                  pl.BlockSpec((SUBLANES, width), lambda i: (jnp.maximum(i * sub_per_tile - 1, 0), 0)),
                  full((1, width)), full((1, RWKV_WIDTH)), full((DECAY_LORA, RWKV_WIDTH)),
                  full((1, RWKV_WIDTH)), full((AAA_LORA, RWKV_WIDTH)), full((GATE_PAD, RWKV_WIDTH)),
                  full((1, RWKV_WIDTH)), full((1, RWKV_WIDTH)), full((1, RWKV_WIDTH))],
        out_specs=[out_spec] * 8,
        out_shape=[out] * 8,
        compiler_params=pltpu.CompilerParams(
            dimension_semantics=("parallel",),
            vmem_limit_bytes=_vmem_limit(blk, temp_bytes=12 * _nbytes((tq, RWKV_WIDTH), F32))),
        name="rwkv_prep",
    )(pr, pr, row(mu), row(w0), w2, row(a0), a2, g2p, row(k_k), row(k_a), row(r_k))


V_TILES = PACK_ROWS // SUBLANES
ACCUMULATORS = 4


def _rwkv_scan_kernel(r_ref, w_ref, k_ref, v_ref, a_ref, b_ref, bonus_ref, g_ref, gain_ref, bias_ref, o_ref,
                      state_ref, rep_ref, y_ref, *, tc, nb):
    @pl.when(pl.program_id(0) == 0)
    def _():
        state_ref[...] = jnp.zeros_like(state_ref)

    lane_group = lax.broadcasted_iota(jnp.int32, (1, 1, LANES), 2) // RWKV_HEADS

    def by_lane_group(pick):
        out = pick(LANE_GROUPS - 1)
        for grp in range(LANE_GROUPS - 2, -1, -1):
            out = jnp.where(lane_group == grp, pick(grp), out)
        return out

    for i, ref in enumerate((r_ref, w_ref, k_ref, a_ref, b_ref)):
        for b in range(nb):
            for half in range(PACK_ROWS // SUBLANES):
                rows = slice(half * SUBLANES, (half + 1) * SUBLANES)
                x = ref[b, :, rows, :]
                rolled = [x] + [pltpu.roll(x, j * RWKV_HEADS, axis=2) for j in range(1, LANE_GROUPS)]
                for q in range(LANE_GROUPS):
                    rep_ref[i, b, :, q, rows, :] = by_lane_group(lambda grp: rolled[(grp - q) % LANE_GROUPS])

    R, W, K, A, B = range(5)

    def bcast(i, b, t, q, row):
        return jnp.broadcast_to(rep_ref[i, b, t, q, row:row + 1, :], (SUBLANES, LANES))

    def tile(g):
        return pl.ds(g * SUBLANES, SUBLANES)

    def total(parts):
        while len(parts) > 1:
            parts = [parts[i] + parts[i + 1] for i in range(0, len(parts), 2)]
        return parts[0]

    columns = [(q, row) for q in range(LANE_GROUPS) for row in range(PACK_ROWS)]

    def step(t, carry):
        for b in range(nb):
            sa = [[None] * ACCUMULATORS for _ in range(V_TILES)]
            for idx, (q, row) in enumerate(columns):
                ab = bcast(A, b, t, q, row)
                for g in range(V_TILES):
                    term = state_ref[b, q, row, tile(g), :] * ab
                    slot = idx % ACCUMULATORS
                    sa[g][slot] = term if sa[g][slot] is None else sa[g][slot] + term
            sa = [total(parts) for parts in sa]
            vt = [v_ref[b, t, tile(g), :] for g in range(V_TILES)]
            y = [[None] * ACCUMULATORS for _ in range(V_TILES)]
            for idx, (q, row) in enumerate(columns):
                wb, bb = bcast(W, b, t, q, row), bcast(B, b, t, q, row)
                kb, rb = bcast(K, b, t, q, row), bcast(R, b, t, q, row)
                for g in range(V_TILES):
                    s = state_ref[b, q, row, tile(g), :] * wb + sa[g] * bb + vt[g] * kb
                    state_ref[b, q, row, tile(g), :] = s
                    slot = idx % ACCUMULATORS
                    y[g][slot] = s * rb if y[g][slot] is None else y[g][slot] + s * rb
            for g in range(V_TILES):
                y_ref[b, t, tile(g), :] = total(y[g])
        return carry

    lax.fori_loop(0, tc, step, 0)

    inv_n = 1.0 / RWKV_HEAD_DIM
    for b in range(nb):
        y = y_ref[b]
        mean = _group_allreduce(jnp.sum(y, axis=1, keepdims=True), 2) * inv_n
        yc = y - mean
        var = _group_allreduce(jnp.sum(yc * yc, axis=1, keepdims=True), 2) * inv_n
        yn = yc * lax.rsqrt(var + GN_EPS)
        out = (yn * gain_ref[...] + bias_ref[...] + bonus_ref[b]) * g_ref[b]
        o_ref[b] = out.astype(o_ref.dtype)


def rwkv_scan(r, w, k, v, a, b, bonus, g, gn_gain, gn_bias, batch, tc=32):
    t_total = r.shape[0]
    s = t_total // batch
    tc = min(tc, s)
    packed = lambda x: x.reshape(batch, s, PACK_ROWS, LANES)
    spec = pl.BlockSpec((batch, tc, PACK_ROWS, LANES), lambda i: (0, i, 0, 0))
    affine = pl.BlockSpec((PACK_ROWS, LANES), lambda i: (0, 0))
    blk = 8 * _nbytes((batch, tc, PACK_ROWS, LANES), F32) + _nbytes((batch, tc, PACK_ROWS, LANES), BF16)
    state_shape = (batch, LANE_GROUPS, PACK_ROWS, PACK_ROWS, LANES)
    rep_shape = (5, batch, tc, LANE_GROUPS, PACK_ROWS, LANES)
    y_shape = (batch, tc, PACK_ROWS, LANES)
    scratch = _nbytes(state_shape, F32) + _nbytes(rep_shape, F32) + _nbytes(y_shape, F32)
    out = pl.pallas_call(
        functools.partial(_rwkv_scan_kernel, tc=tc, nb=batch),
        grid=(s // tc,),
        in_specs=[spec] * 8 + [affine, affine],
        out_specs=spec,
        out_shape=jax.ShapeDtypeStruct((batch, s, PACK_ROWS, LANES), BF16),
        scratch_shapes=[pltpu.VMEM(state_shape, F32), pltpu.VMEM(rep_shape, F32), pltpu.VMEM(y_shape, F32)],
        compiler_params=pltpu.CompilerParams(
            dimension_semantics=("arbitrary",),
            vmem_limit_bytes=_vmem_limit(blk, scratch_bytes=scratch, temp_bytes=8 * _nbytes(y_shape, F32))),
        name="rwkv_scan",
    )(packed(r), packed(w), packed(k), packed(v), packed(a), packed(b), packed(bonus), packed(g),
      gn_gain.reshape(PACK_ROWS, LANES), gn_bias.reshape(PACK_ROWS, LANES))
    return out.reshape(t_total, RWKV_WIDTH)


def _mem_attn_kernel(q_ref, k_ref, v_ref, qg_ref, kg_ref, o_ref):
    scale = MEM_HEAD_DIM ** -0.5
    for h in range(MEM_HEADS):
        cols = slice(h * MEM_HEAD_DIM, (h + 1) * MEM_HEAD_DIM)
        q = q_ref[:, cols]
        q = q * lax.rsqrt(jnp.mean(q * q, axis=-1, keepdims=True) + NORM_EPS) * qg_ref[...]
        k = k_ref[:, cols]
        k = k * lax.rsqrt(jnp.mean(k * k, axis=-1, keepdims=True) + NORM_EPS) * kg_ref[...]
        s = lax.dot_general(q.astype(BF16), k.astype(BF16), (((1,), (1,)), ((), ())),
                            preferred_element_type=F32) * scale
        p = jnp.exp(s - jnp.max(s, axis=-1, keepdims=True))
        p = p / jnp.sum(p, axis=-1, keepdims=True)
        o = jnp.dot(p.astype(BF16), v_ref[:, cols].astype(BF16), preferred_element_type=F32)
        o_ref[:, cols] = o.astype(o_ref.dtype)


def mem_attention(q3, k3, v3, q_gain, k_gain, tq=512):
    b, s, _ = q3.shape
    m = k3.shape[1]
    tq = min(tq, s)
    blk = _nbytes((tq, MEM_WIDTH), F32) + 2 * _nbytes((m, MEM_WIDTH), F32) + _nbytes((tq, MEM_WIDTH), BF16)
    return pl.pallas_call(
        _mem_attn_kernel,
        grid=(b, s // tq),
        in_specs=[pl.BlockSpec((None, tq, MEM_WIDTH), lambda bi, i: (bi, i, 0)),
                  pl.BlockSpec((None, m, MEM_WIDTH), lambda bi, i: (bi, 0, 0)),
                  pl.BlockSpec((None, m, MEM_WIDTH), lambda bi, i: (bi, 0, 0)),
                  pl.BlockSpec((1, MEM_HEAD_DIM), lambda bi, i: (0, 0)),
                  pl.BlockSpec((1, MEM_HEAD_DIM), lambda bi, i: (0, 0))],
        out_specs=pl.BlockSpec((None, tq, MEM_WIDTH), lambda bi, i: (bi, i, 0)),
        out_shape=jax.ShapeDtypeStruct((b, s, MEM_WIDTH), BF16),
        compiler_params=pltpu.CompilerParams(
            dimension_semantics=("parallel", "parallel"),
            vmem_limit_bytes=_vmem_limit(blk, temp_bytes=8 * _nbytes((tq, m), F32))),
        name="mem_attention",
    )(q3, k3, v3, q_gain.reshape(1, -1), k_gain.reshape(1, -1))


def _projection_tail_kernel(w_ref, o_ref, *, tn, valid):
    col = pl.program_id(0) * tn + lax.broadcasted_iota(jnp.int32, w_ref.shape, 1)
    o_ref[...] = jnp.where(col < valid, w_ref[...], 0.0).astype(o_ref.dtype)


def projection_tail(w_in, layer, tn=512):
    _, kdim, total = w_in.shape
    return pl.pallas_call(
        functools.partial(_projection_tail_kernel, tn=tn, valid=total - FOX_QKVG),
        grid=(RWKV_PAD // tn,),
        in_specs=[pl.BlockSpec((None, kdim, tn), lambda j: (layer, 0, FOX_QKVG // tn + j))],
        out_specs=pl.BlockSpec((kdim, tn), lambda j: (0, j)),
        out_shape=jax.ShapeDtypeStruct((kdim, RWKV_PAD), BF16),
        compiler_params=pltpu.CompilerParams(
            dimension_semantics=("parallel",),
            vmem_limit_bytes=_vmem_limit(_nbytes((kdim, tn), F32) + _nbytes((kdim, tn), BF16),
                                         temp_bytes=_nbytes((kdim, tn), F32))),
        name="projection_tail",
    )(w_in)


def _rwkv_projection_weight(w_in, layer):
    tail = projection_tail(w_in, layer)

    def cols(lo, hi):
        return tail[:, FOX_HEADS + lo:FOX_HEADS + hi]

    parts = [_to_column_layout(cols(R_OFF, WLO_OFF)), cols(WLO_OFF, K_OFF), _to_column_layout(cols(K_OFF, V_OFF)),
             _to_value_layout(cols(V_OFF, ALO_OFF)), cols(ALO_OFF, RWKV_IN),
             jnp.zeros((w_in.shape[1], RWKV_PAD - RWKV_IN), BF16)]
    return jnp.concatenate(parts, axis=1)


def _rwkv_shift_mix(mu):
    parts = [_to_column_layout(mu[R_OFF:WLO_OFF]), mu[WLO_OFF:K_OFF], _to_column_layout(mu[K_OFF:V_OFF]),
             _to_value_layout(mu[V_OFF:ALO_OFF]), mu[ALO_OFF:], jnp.zeros((RWKV_READ - RWKV_IN,), mu.dtype)]
    return jnp.concatenate(parts)


def _layer(x, mem2, b, s, layer, p, w):
    t_total = b * s
    h = rmsnorm_bf16(x, p["norm_mix"])
    pf = matmul(h, w["w_in"], layer=layer, n=FOX_QKVG, tm=2048, single_buffer_a=True)
    pr = matmul(h, _rwkv_projection_weight(w["w_in"], layer), tm=2048, single_buffer_a=True)

    pf3 = pf.reshape(b, s, FOX_QKVG)
    qn, kn, vb, c = fox_prep(pf3, h.reshape(b, s, D_MODEL), w["w_in"], layer, p["fox_f_bias"],
                             p["fox_q_gain"], p["fox_k_gain"])
    y_fox = fox_attention(qn, kn, vb, c, pf3).reshape(t_total, FOX_WIDTH)

    g2p = _to_value_layout(jnp.pad(p["rwkv_g2"], ((0, GATE_PAD - GATE_LORA), (0, 0)))).astype(BF16)
    r, wd, k, v, a, bb, g, bonus = rwkv_prep(
        pr, s, _rwkv_shift_mix(p["rwkv_mu"]), _to_column_layout(p["rwkv_w0"]), _to_column_layout(p["rwkv_w2"]),
        _to_column_layout(p["rwkv_a0"]), _to_column_layout(p["rwkv_a2"]), g2p, _to_column_layout(p["rwkv_k_k"]),
        _to_column_layout(p["rwkv_k_a"]), _to_column_layout(p["rwkv_r_k"].reshape(-1)))
    y_rwkv = rwkv_scan(r, wd, k, v, a, bb, bonus, g, _to_value_layout(p["rwkv_gn_gain"]),
                       _to_value_layout(p["rwkv_gn_bias"]), b)

    w_rwkv_rows = _to_value_layout(w["w_out"][layer, FOX_WIDTH:, :], axis=0)
    x = out_proj(y_fox, y_rwkv, w["w_out"], layer, w_rwkv_rows, x)

    h = rmsnorm_bf16(x, p["norm_mem_q"])
    m = rmsnorm_bf16(mem2, p["norm_mem_kv"])
    q = matmul(h, w["mem_w_q"], layer=layer)
    km = matmul(m, w["mem_w_k"], layer=layer)
    vm = matmul(m, w["mem_w_v"], layer=layer)
    n_mem = mem2.shape[0] // b
    o = mem_attention(q.reshape(b, s, MEM_WIDTH), km.reshape(b, n_mem, MEM_WIDTH),
                      vm.reshape(b, n_mem, MEM_WIDTH), p["mem_q_gain"], p["mem_k_gain"])
    x = matmul(o.reshape(t_total, MEM_WIDTH), w["mem_w_o"], layer=layer, epilogue="residual", residual=x,
               tn=1024)

    h = rmsnorm_bf16(x, p["norm_mlp"])
    u = matmul(h, w["w_up"], layer=layer, epilogue="relu2", out_dtype=BF16, tm=2048, single_buffer_a=True)
    x = matmul(u, w["w_down"], layer=layer, epilogue="residual", residual=x, tm=1024, tn=1024, tk=2048)
    return x


_PARAM_NAMES = ("norm_mix", "w_in", "fox_q_gain", "fox_k_gain", "fox_f_bias", "rwkv_mu", "rwkv_w0", "rwkv_w2",
                "rwkv_a0", "rwkv_a2", "rwkv_g2", "rwkv_k_k", "rwkv_k_a", "rwkv_r_k", "rwkv_gn_gain",
                "rwkv_gn_bias", "w_out", "norm_mem_q", "norm_mem_kv", "mem_w_q", "mem_w_k", "mem_w_v",
                "mem_q_gain", "mem_k_gain", "mem_w_o", "norm_mlp", "w_up", "w_down")
_STACKED_WEIGHTS = ("w_in", "w_out", "mem_w_q", "mem_w_k", "mem_w_v", "mem_w_o", "w_up", "w_down")


def kernel(x, mem, norm_mix, w_in, fox_q_gain, fox_k_gain, fox_f_bias, rwkv_mu, rwkv_w0, rwkv_w2, rwkv_a0, rwkv_a2, rwkv_g2, rwkv_k_k, rwkv_k_a, rwkv_r_k, rwkv_gn_gain, rwkv_gn_bias, w_out, norm_mem_q, norm_mem_kv, mem_w_q, mem_w_k, mem_w_v, mem_q_gain, mem_k_gain, mem_w_o, norm_mlp, w_up, w_down):
    params = dict(zip(_PARAM_NAMES, (norm_mix, w_in, fox_q_gain, fox_k_gain, fox_f_bias, rwkv_mu, rwkv_w0,
                                     rwkv_w2, rwkv_a0, rwkv_a2, rwkv_g2, rwkv_k_k, rwkv_k_a, rwkv_r_k,
                                     rwkv_gn_gain, rwkv_gn_bias, w_out, norm_mem_q, norm_mem_kv, mem_w_q,
                                     mem_w_k, mem_w_v, mem_q_gain, mem_k_gain, mem_w_o, norm_mlp, w_up, w_down)))
    b, s, d = x.shape
    assert d == D_MODEL, x.shape
    depth = w_in.shape[0]
    stacked = {name: params[name] for name in _STACKED_WEIGHTS}
    x2 = x.reshape(b * s, d)
    mem2 = mem.reshape(-1, d)
    for layer in range(depth):
        small = {name: value[layer] for name, value in params.items() if name not in _STACKED_WEIGHTS}
        x2 = _layer(x2, mem2, b, s, layer, small, stacked)
    return x2.reshape(b, s, d)
```

```python
import functools

import jax
import jax.numpy as jnp
from jax import lax
from jax.experimental import pallas as pl
from jax.experimental.pallas import tpu as pltpu

D_MODEL = 4096
FOX_WIDTH = 2048
FOX_HEAD_DIM = 128
FOX_HEADS = 16
RWKV_WIDTH = 2048
RWKV_HEAD_DIM = 64
RWKV_HEADS = 32
DECAY_LORA = 128
AAA_LORA = 128
GATE_LORA = 480
MEM_HEADS = 4
MEM_HEAD_DIM = 128
MEM_WIDTH = MEM_HEADS * MEM_HEAD_DIM
NORM_EPS = 1e-6
GN_EPS = 64e-5
FOX_QKVG = 4 * FOX_WIDTH
FOX_IN = FOX_QKVG + FOX_HEADS
RWKV_IN = 3 * RWKV_WIDTH + DECAY_LORA + AAA_LORA + GATE_LORA
R_OFF, WLO_OFF, K_OFF, V_OFF, ALO_OFF, GLO_OFF = 0, 2048, 2176, 4224, 6272, 6400

LANES = 128
SUBLANES = 8
VMEM_BYTES_V7X = 64 * 1024 * 1024
RWKV_READ = 6912
RWKV_PAD = 7168
GATE_PAD = RWKV_READ - GLO_OFF

LANE_GROUPS = LANES // RWKV_HEADS
PACK_ROWS = RWKV_WIDTH // LANES

F32 = jnp.float32
BF16 = jnp.bfloat16
HIGHEST = lax.Precision.HIGHEST
LOG2_E = 1.4426950408889634


def _vmem_limit(block_bytes, scratch_bytes=0, temp_bytes=0):
    need = 2 * block_bytes + scratch_bytes + temp_bytes + (4 << 20)
    return int(min(need, VMEM_BYTES_V7X - (6 << 20)))


def _nbytes(shape, dtype):
    n = 1
    for s in shape:
        n *= s
    return n * jnp.dtype(dtype).itemsize


def _rmsnorm_kernel(x_ref, g_ref, o_ref):
    x = x_ref[...]
    ms = jnp.mean(x * x, axis=-1, keepdims=True)
    o_ref[...] = (x * lax.rsqrt(ms + NORM_EPS) * g_ref[...]).astype(o_ref.dtype)


def rmsnorm_bf16(x, gain, tm=256):
    m, d = x.shape
    tm = min(tm, m)
    return pl.pallas_call(
        _rmsnorm_kernel,
        grid=(m // tm,),
        in_specs=[pl.BlockSpec((tm, d), lambda i: (i, 0)),
                  pl.BlockSpec((1, d), lambda i: (0, 0))],
        out_specs=pl.BlockSpec((tm, d), lambda i: (i, 0)),
        out_shape=jax.ShapeDtypeStruct((m, d), BF16),
        compiler_params=pltpu.CompilerParams(
            dimension_semantics=("parallel",),
            vmem_limit_bytes=_vmem_limit(_nbytes((tm, d), F32) + _nbytes((tm, d), BF16),
                                         temp_bytes=2 * _nbytes((tm, d), F32))),
        name="rmsnorm",
    )(x, gain.reshape(1, d))


def _matmul_kernel(*refs, epilogue, nk, w_transposed):
    if epilogue == "residual":
        a_ref, w_ref, r_ref, o_ref = refs
    else:
        a_ref, w_ref, o_ref = refs

    def product():
        w = w_ref[...].astype(BF16)
        if w_transposed:
            return lax.dot_general(a_ref[...], w, (((1,), (1,)), ((), ())), preferred_element_type=F32)
        return jnp.dot(a_ref[...], w, preferred_element_type=F32)

    if nk == 1:
        d = product()
        if epilogue == "relu2":
            r = jnp.maximum(d, 0.0)
            o_ref[...] = (r * r).astype(o_ref.dtype)
        elif epilogue == "residual":
            o_ref[...] = r_ref[...] + d
        else:
            o_ref[...] = d.astype(o_ref.dtype)
    else:
        k = pl.program_id(2)

        @pl.when(k == 0)
        def _():
            o_ref[...] = r_ref[...] + product() if epilogue == "residual" else product()

        @pl.when(k > 0)
        def _():
            o_ref[...] += product()


def matmul(a, w, *, layer=None, n=None, epilogue="none", residual=None, out_dtype=F32,
           tm=1024, tn=512, tk=None, single_buffer_a=False, w_transposed=False):
    m, kdim = a.shape
    n = w.shape[-2 if w_transposed else -1] if n is None else n
    tk = kdim if tk is None else tk
    tm, tn = min(tm, m), min(tn, n)
    assert m % tm == 0 and n % tn == 0 and kdim % tk == 0, (a.shape, w.shape, tm, tn, tk)
    nk = kdim // tk
    assert nk == 1 or (out_dtype == F32 and epilogue != "relu2")
    w_block = (tn, tk) if w_transposed else (tk, tn)
    w_index = (lambda k, j: (j, k)) if w_transposed else (lambda k, j: (k, j))
    if layer is None:
        w_spec = pl.BlockSpec(w_block, lambda i, j, k: w_index(k, j))
    else:
        w_spec = pl.BlockSpec((None,) + w_block, lambda i, j, k: (layer,) + w_index(k, j))
    a_mode = pl.Buffered(1) if single_buffer_a else None
    in_specs = [pl.BlockSpec((tm, tk), lambda i, j, k: (i, k), pipeline_mode=a_mode), w_spec]
    args = [a, w]
    a_bytes = _nbytes((tm, tk), BF16)
    blk = (a_bytes // 2 if single_buffer_a else a_bytes) + _nbytes((tk, tn), w.dtype) + _nbytes((tm, tn), out_dtype)
    if epilogue == "residual":
        in_specs.append(pl.BlockSpec((tm, tn), lambda i, j, k: (i, j)))
        args.append(residual)
        blk += _nbytes((tm, tn), F32)
    return pl.pallas_call(
        functools.partial(_matmul_kernel, epilogue=epilogue, nk=nk, w_transposed=w_transposed),
        grid=(m // tm, n // tn, nk),
        in_specs=in_specs,
        out_specs=pl.BlockSpec((tm, tn), lambda i, j, k: (i, j)),
        out_shape=jax.ShapeDtypeStruct((m, n), out_dtype),
        compiler_params=pltpu.CompilerParams(
            dimension_semantics=("parallel", "parallel", "arbitrary"),
            vmem_limit_bytes=_vmem_limit(blk, temp_bytes=_nbytes((tk, tn), BF16) + 2 * _nbytes((tm, tn), F32))),
        name="matmul_" + epilogue,
    )(*args)


def _out_proj_kernel(a1_ref, a2_ref, w1_ref, w2_ref, r_ref, o_ref):
    d = jnp.dot(a1_ref[...], w1_ref[...].astype(BF16), preferred_element_type=F32)
    d = d + jnp.dot(a2_ref[...], w2_ref[...].astype(BF16), preferred_element_type=F32)
    o_ref[...] = r_ref[...] + d


def out_proj(y_fox, y_rwkv, w_out, layer, w_rwkv_rows, x, tm=1024, tn=512):
    m, half = y_fox.shape
    n = x.shape[1]
    tm = min(tm, m)
    assert m % tm == 0 and n % tn == 0, (m, n, tm, tn)
    blk = 2 * _nbytes((tm, half), BF16) + 2 * _nbytes((half, tn), F32) + 2 * _nbytes((tm, tn), F32)
    return pl.pallas_call(
        _out_proj_kernel,
        grid=(m // tm, n // tn),
        in_specs=[pl.BlockSpec((tm, half), lambda i, j: (i, 0)),
                  pl.BlockSpec((tm, half), lambda i, j: (i, 0)),
                  pl.BlockSpec((None, half, tn), lambda i, j: (layer, 0, j)),
                  pl.BlockSpec((half, tn), lambda i, j: (0, j)),
                  pl.BlockSpec((tm, tn), lambda i, j: (i, j))],
        out_specs=pl.BlockSpec((tm, tn), lambda i, j: (i, j)),
        out_shape=jax.ShapeDtypeStruct((m, n), F32),
        compiler_params=pltpu.CompilerParams(
            dimension_semantics=("parallel", "parallel"),
            vmem_limit_bytes=_vmem_limit(blk, temp_bytes=2 * _nbytes((half, tn), BF16) + 2 * _nbytes((tm, tn), F32))),
        name="out_proj",
    )(y_fox, y_rwkv, w_out, w_rwkv_rows, x)


def _fox_prep_kernel(p_ref, h_ref, wf_ref, fb_ref, qg_ref, kg_ref, tri_ref,
                     q_o, k_o, c_o, carry_ref, *, tq):
    @pl.when(pl.program_id(1) == 0)
    def _():
        carry_ref[...] = jnp.zeros_like(carry_ref)

    scale = FOX_HEAD_DIM ** -0.5 * LOG2_E
    for h in range(FOX_HEADS):
        lo = h * FOX_HEAD_DIM
        q = p_ref[:, lo:lo + FOX_HEAD_DIM].astype(F32)
        ms = jnp.mean(q * q, axis=-1, keepdims=True)
        q_o[:, lo:lo + FOX_HEAD_DIM] = (q * lax.rsqrt(ms + NORM_EPS) * qg_ref[...] * scale).astype(BF16)
        k = p_ref[:, FOX_WIDTH + lo:FOX_WIDTH + lo + FOX_HEAD_DIM].astype(F32)
        ms = jnp.mean(k * k, axis=-1, keepdims=True)
        k_o[:, lo:lo + FOX_HEAD_DIM] = (k * lax.rsqrt(ms + NORM_EPS) * kg_ref[...]).astype(BF16)

    f_logit = lax.dot_general(h_ref[...], wf_ref[...].astype(BF16), (((1,), (1,)), ((), ())),
                              preferred_element_type=F32)
    log_f = jax.nn.log_sigmoid(f_logit + fb_ref[...])
    c = jnp.dot(tri_ref[...], log_f, precision=HIGHEST, preferred_element_type=F32) + carry_ref[0:1, :]
    c_o[...] = c * LOG2_E
    carry_ref[...] = jnp.broadcast_to(c[tq - 1:tq, :], carry_ref.shape)


def fox_prep(pf3, h3, w_in_t, layer, f_bias, q_gain, k_gain, tq=512):
    b, s, _ = pf3.shape
    tq = min(tq, s)
    tri = (lax.broadcasted_iota(jnp.int32, (tq, tq), 0) >= lax.broadcasted_iota(jnp.int32, (tq, tq), 1)).astype(F32)
    qk_w = 2 * FOX_WIDTH
    act = jax.ShapeDtypeStruct((b, s, FOX_WIDTH), BF16)
    fb = jnp.pad(f_bias, (0, LANES - FOX_HEADS)).reshape(1, LANES)
    blk = (_nbytes((tq, qk_w), BF16) + _nbytes((tq, D_MODEL), BF16) + _nbytes((LANES, D_MODEL), F32)
           + _nbytes((tq, tq), F32) + 2 * _nbytes((tq, FOX_WIDTH), BF16) + _nbytes((tq, LANES), F32))
    qn, kn, c = pl.pallas_call(
        functools.partial(_fox_prep_kernel, tq=tq),
        grid=(b, s // tq),
        in_specs=[pl.BlockSpec((None, tq, qk_w), lambda bi, i: (bi, i, 0)),
                  pl.BlockSpec((None, tq, D_MODEL), lambda bi, i: (bi, i, 0)),
                  pl.BlockSpec((None, LANES, D_MODEL), lambda bi, i: (layer, FOX_QKVG // LANES, 0)),
                  pl.BlockSpec((1, LANES), lambda bi, i: (0, 0)),
                  pl.BlockSpec((1, FOX_HEAD_DIM), lambda bi, i: (0, 0)),
                  pl.BlockSpec((1, FOX_HEAD_DIM), lambda bi, i: (0, 0)),
                  pl.BlockSpec((tq, tq), lambda bi, i: (0, 0))],
        out_specs=[pl.BlockSpec((None, tq, FOX_WIDTH), lambda bi, i: (bi, i, 0)),
                   pl.BlockSpec((None, tq, FOX_WIDTH), lambda bi, i: (bi, i, 0)),
                   pl.BlockSpec((None, tq, LANES), lambda bi, i: (bi, i, 0))],
        out_shape=[act, act, jax.ShapeDtypeStruct((b, s, LANES), F32)],
        scratch_shapes=[pltpu.VMEM((SUBLANES, LANES), F32)],
        compiler_params=pltpu.CompilerParams(
            dimension_semantics=("parallel", "arbitrary"),
            vmem_limit_bytes=_vmem_limit(blk, temp_bytes=_nbytes((tq, qk_w), F32))),
        name="fox_prep",
    )(pf3, h3, w_in_t, fb, q_gain.reshape(1, -1), k_gain.reshape(1, -1), tri)
    return qn, kn, c[:, :, :FOX_HEADS].transpose(0, 2, 1)


FOX_HEADS_PER_STEP = 1


def _fox_attn_kernel(q_ref, k_ref, v_ref, c_ref, gate_ref, o_ref, m_ref, l_ref, acc_ref, s_ref, p_ref, *, t):
    qi = pl.program_id(2)
    heads = range(FOX_HEADS_PER_STEP)
    m_ref[...] = jnp.full_like(m_ref, -jnp.inf)
    l_ref[...] = jnp.zeros_like(l_ref)
    acc_ref[...] = jnp.zeros_like(acc_ref)
    for hh in heads:
        p_ref[hh, 1] = jnp.zeros((t, t), BF16)

    def cols(hh):
        return slice(hh * FOX_HEAD_DIM, (hh + 1) * FOX_HEAD_DIM)

    def rows(ref, blk, hh):
        return ref[pl.ds(pl.multiple_of(blk * t, t), t), cols(hh)]

    c_base = [c_ref[hh, pl.ds(qi, 1), :][:, 0:1] for hh in heads]

    def scores(ki, hh):
        s = lax.dot_general(q_ref[:, cols(hh)], rows(k_ref, ki, hh), (((1,), (1,)), ((), ())),
                            preferred_element_type=F32)
        return s + (c_base[hh] - c_ref[hh, pl.ds(ki, 1), :])

    def weighted_values(slot, blk, hh):
        return jnp.dot(p_ref[hh, slot], rows(v_ref, blk, hh), preferred_element_type=F32)

    def step(ki, masked):
        for hh in heads:
            s = s_ref[hh, ki % 2]
            pv_prev = weighted_values((ki + 1) % 2, jnp.maximum(ki - 1, 0), hh)
            if not masked:
                s_ref[hh, (ki + 1) % 2] = scores(ki + 1, hh)
            else:
                row = lax.broadcasted_iota(jnp.int32, (t, t), 0)
                col = lax.broadcasted_iota(jnp.int32, (t, t), 1)
                s = jnp.where(col <= row, s, -jnp.inf)
            m_prev = m_ref[hh]
            m_new = jnp.maximum(m_prev, jnp.max(s, axis=1, keepdims=True))
            alpha = jnp.exp2(m_prev - m_new)
            p = jnp.exp2(s - jnp.tile(m_new, (1, t // LANES)))
            l_ref[hh] = alpha * l_ref[hh] + jnp.sum(p, axis=1, keepdims=True)
            acc_ref[hh] = alpha * (acc_ref[hh] + pv_prev)
            m_ref[hh] = m_new
            p_ref[hh, ki % 2] = p.astype(BF16)

    for hh in heads:
        s_ref[hh, 0] = scores(0, hh)

    def body(ki, carry):
        step(ki, False)
        return carry

    lax.fori_loop(0, qi, body, 0)
    step(qi, True)
    for hh in heads:
        o = (acc_ref[hh] + weighted_values(qi % 2, qi, hh)) / l_ref[hh]
        o_ref[:, cols(hh)] = (o * jax.nn.sigmoid(gate_ref[:, cols(hh)].astype(F32))).astype(o_ref.dtype)


def fox_attention(qn, kn, c, pf3, t=512):
    b, s, _ = qn.shape
    t = min(t, s)
    nt = s // t
    hp = FOX_HEADS_PER_STEP
    width = hp * FOX_HEAD_DIM
    c4 = c.reshape(b, FOX_HEADS, nt, t)
    value_blk0 = 2 * FOX_WIDTH // width
    gate_blk0 = 3 * FOX_WIDTH // width
    blk = 3 * _nbytes((t, width), BF16) + 2 * _nbytes((s, width), BF16) + hp * _nbytes((nt, t), F32)
    scratch = hp * (3 * _nbytes((t, LANES), F32) + 3 * _nbytes((t, t), F32))
    return pl.pallas_call(
        functools.partial(_fox_attn_kernel, t=t),
        grid=(b, FOX_HEADS // hp, nt),
        in_specs=[pl.BlockSpec((None, t, width), lambda bi, h, qi: (bi, qi, h)),
                  pl.BlockSpec((None, s, width), lambda bi, h, qi: (bi, 0, h)),
                  pl.BlockSpec((None, s, width), lambda bi, h, qi: (bi, 0, value_blk0 + h)),
                  pl.BlockSpec((None, hp, nt, t), lambda bi, h, qi: (bi, h, 0, 0)),
                  pl.BlockSpec((None, t, width), lambda bi, h, qi: (bi, qi, gate_blk0 + h))],
        out_specs=pl.BlockSpec((None, t, width), lambda bi, h, qi: (bi, qi, h)),
        out_shape=jax.ShapeDtypeStruct((b, s, FOX_WIDTH), BF16),
        scratch_shapes=[pltpu.VMEM((hp, t, LANES), F32), pltpu.VMEM((hp, t, LANES), F32),
                        pltpu.VMEM((hp, t, FOX_HEAD_DIM), F32),
                        pltpu.VMEM((hp, 2, t, t), F32), pltpu.VMEM((hp, 2, t, t), BF16)],
        compiler_params=pltpu.CompilerParams(
            dimension_semantics=("parallel", "parallel", "arbitrary"),
            vmem_limit_bytes=_vmem_limit(blk, scratch_bytes=scratch, temp_bytes=8 * _nbytes((t, t), F32))),
        name="fox_attention",
    )(qn, kn, pf3, c4, pf3)


def _to_column_layout(x, axis=-1):
    x = jnp.moveaxis(x, axis, -1)
    y = x.reshape(x.shape[:-1] + (RWKV_HEADS, RWKV_HEAD_DIM)).swapaxes(-1, -2).reshape(x.shape)
    return jnp.moveaxis(y, -1, axis)


def _to_value_layout(x, axis=-1):
    x = jnp.moveaxis(x, axis, -1)
    lead = x.ndim - 1
    y = x.reshape(x.shape[:-1] + (RWKV_HEADS, 2, LANE_GROUPS, SUBLANES))
    y = y.transpose(tuple(range(lead)) + (lead + 1, lead + 3, lead + 2, lead)).reshape(x.shape)
    return jnp.moveaxis(y, -1, axis)


def _group_allreduce(x, axis):
    x = x + pltpu.roll(x, 2 * RWKV_HEADS, axis=axis)
    return x + pltpu.roll(x, RWKV_HEADS, axis=axis)


def _head_sum(x):
    s = x[:, 0:LANES]
    for row in range(1, PACK_ROWS):
        s = s + x[:, row * LANES:(row + 1) * LANES]
    return _group_allreduce(s, 1)


def _rwkv_prep_kernel(p_ref, pprev_ref, mu_ref, w0_ref, w2_ref, a0_ref, a2_ref, g2_ref, kk_ref, ka_ref,
                      rk_ref, r_o, w_o, k_o, v_o, a_o, b_o, g_o, bonus_o, *, tiles_per_seq):
    first = (pl.program_id(0) % tiles_per_seq) == 0

    def shifted(lo, hi):
        p = p_ref[:, lo:hi]
        prev_row = jnp.where(first, 0.0, pprev_ref[SUBLANES - 1:SUBLANES, lo:hi])
        row = lax.broadcasted_iota(jnp.int32, p.shape, 0)
        prev = jnp.where(row == 0, prev_row, pltpu.roll(p, 1, axis=0))
        return p + (prev - p) * mu_ref[:, lo:hi]

    def over_rows(s):
        return jnp.tile(s, (1, PACK_ROWS))

    w_lo = shifted(WLO_OFF, WLO_OFF + DECAY_LORA)
    z = w0_ref[...] + jnp.dot(jnp.tanh(w_lo), w2_ref[...], precision=HIGHEST, preferred_element_type=F32)
    softplus_neg = jnp.maximum(-z, 0.0) + jnp.log1p(jnp.exp(-jnp.abs(z)))
    w_o[...] = jnp.exp(-jnp.exp(-softplus_neg - 0.5))

    a_lo = shifted(ALO_OFF, ALO_OFF + AAA_LORA)
    a_lr = jax.nn.sigmoid(a0_ref[...] + jnp.dot(a_lo, a2_ref[...], precision=HIGHEST,
                                                preferred_element_type=F32))

    g_lo = shifted(GLO_OFF, GLO_OFF + GATE_PAD)
    g_o[...] = jnp.dot(jax.nn.sigmoid(g_lo).astype(BF16), g2_ref[...], preferred_element_type=F32)

    k = shifted(K_OFF, K_OFF + RWKV_WIDTH)
    kk = k * kk_ref[...]
    kk = kk * over_rows(lax.rsqrt(jnp.maximum(_head_sum(kk * kk), 1e-24)))
    a_o[...] = -kk
    b_o[...] = kk * a_lr
    k = k * (1.0 + (a_lr - 1.0) * ka_ref[...])
    k_o[...] = k

    r = shifted(R_OFF, R_OFF + RWKV_WIDTH)
    r_o[...] = r
    v = shifted(V_OFF, V_OFF + RWKV_WIDTH)
    v_o[...] = v
    bonus_o[...] = over_rows(_head_sum(r * k * rk_ref[...])) * v


def rwkv_prep(pr, seq, mu, w0, w2, a0, a2, g2p, k_k, k_a, r_k, tq=128):
    t_total = pr.shape[0]
    width = RWKV_READ
    tq = min(tq, seq)
    row = lambda x: x.reshape(1, -1)
    full = lambda shape: pl.BlockSpec(shape, lambda i: (0, 0))
    out = jax.ShapeDtypeStruct((t_total, RWKV_WIDTH), F32)
    out_spec = pl.BlockSpec((tq, RWKV_WIDTH), lambda i: (i, 0))
    sub_per_tile = tq // SUBLANES
    blk = (_nbytes((tq, width), F32) + _nbytes((SUBLANES, width), F32) + 8 * _nbytes((tq, RWKV_WIDTH), F32)
           + _nbytes((DECAY_LORA + AAA_LORA, RWKV_WIDTH), F32) + _nbytes((GATE_PAD, RWKV_WIDTH), BF16))
    return pl.pallas_call(
        functools.partial(_rwkv_prep_kernel, tiles_per_seq=seq // tq),
        grid=(t_total // tq,),
        in_specs=[pl.BlockSpec((tq, width), lambda i: (i, 0)),
                  pl.BlockSpec((SUBLANES, width), lambda i: (jnp.maximum(i * sub_per_tile - 1, 0), 0)),
                  full((1, width)), full((1, RWKV_WIDTH)), full((DECAY_LORA, RWKV_WIDTH)),
                  full((1, RWKV_WIDTH)), full((AAA_LORA, RWKV_WIDTH)), full((GATE_PAD, RWKV_WIDTH)),
                  full((1, RWKV_WIDTH)), full((1, RWKV_WIDTH)), full((1, RWKV_WIDTH))],
        out_specs=[out_spec] * 8,
        out_shape=[out] * 8,
        compiler_params=pltpu.CompilerParams(
            dimension_semantics=("parallel",),
            vmem_limit_bytes=_vmem_limit(blk, temp_bytes=12 * _nbytes((tq, RWKV_WIDTH), F32))),
        name="rwkv_prep",
    )(pr, pr, row(mu), row(w0), w2, row(a0), a2, g2p, row(k_k), row(k_a), row(r_k))


V_TILES = PACK_ROWS // SUBLANES
ACCUMULATORS = 4


def _rwkv_scan_kernel(r_ref, w_ref, k_ref, v_ref, a_ref, b_ref, bonus_ref, g_ref, gain_ref, bias_ref, o_ref,
                      state_ref, rep_ref, y_ref, *, tc, nb):
    @pl.when(pl.program_id(0) == 0)
    def _():
        state_ref[...] = jnp.zeros_like(state_ref)

    lane_group = lax.broadcasted_iota(jnp.int32, (1, 1, LANES), 2) // RWKV_HEADS

    def by_lane_group(pick):
        out = pick(LANE_GROUPS - 1)
        for grp in range(LANE_GROUPS - 2, -1, -1):
            out = jnp.where(lane_group == grp, pick(grp), out)
        return out

    for i, ref in enumerate((r_ref, w_ref, k_ref, a_ref, b_ref)):
        for b in range(nb):
            for half in range(PACK_ROWS // SUBLANES):
                rows = slice(half * SUBLANES, (half + 1) * SUBLANES)
                x = ref[b, :, rows, :]
                rolled = [x] + [pltpu.roll(x, j * RWKV_HEADS, axis=2) for j in range(1, LANE_GROUPS)]
                for q in range(LANE_GROUPS):
                    rep_ref[i, b, :, q, rows, :] = by_lane_group(lambda grp: rolled[(grp - q) % LANE_GROUPS])

    R, W, K, A, B = range(5)

    def bcast(i, b, t, q, row):
        return jnp.broadcast_to(rep_ref[i, b, t, q, row:row + 1, :], (SUBLANES, LANES))

    def tile(g):
        return pl.ds(g * SUBLANES, SUBLANES)

    def total(parts):
        while len(parts) > 1:
            parts = [parts[i] + parts[i + 1] for i in range(0, len(parts), 2)]
        return parts[0]

    columns = [(q, row) for q in range(LANE_GROUPS) for row in range(PACK_ROWS)]

    def step(t, carry):
        for b in range(nb):
            sa = [[None] * ACCUMULATORS for _ in range(V_TILES)]
            for idx, (q, row) in enumerate(columns):
                ab = bcast(A, b, t, q, row)
                for g in range(V_TILES):
                    term = state_ref[b, q, row, tile(g), :] * ab
                    slot = idx % ACCUMULATORS
                    sa[g][slot] = term if sa[g][slot] is None else sa[g][slot] + term
            sa = [total(parts) for parts in sa]
            vt = [v_ref[b, t, tile(g), :] for g in range(V_TILES)]
            y = [[None] * ACCUMULATORS for _ in range(V_TILES)]
            for idx, (q, row) in enumerate(columns):
                wb, bb = bcast(W, b, t, q, row), bcast(B, b, t, q, row)
                kb, rb = bcast(K, b, t, q, row), bcast(R, b, t, q, row)
                for g in range(V_TILES):
                    s = state_ref[b, q, row, tile(g), :] * wb + sa[g] * bb + vt[g] * kb
                    state_ref[b, q, row, tile(g), :] = s
                    slot = idx % ACCUMULATORS
                    y[g][slot] = s * rb if y[g][slot] is None else y[g][slot] + s * rb
            for g in range(V_TILES):
                y_ref[b, t, tile(g), :] = total(y[g])
        return carry

    lax.fori_loop(0, tc, step, 0)

    inv_n = 1.0 / RWKV_HEAD_DIM
    for b in range(nb):
        y = y_ref[b]
        mean = _group_allreduce(jnp.sum(y, axis=1, keepdims=True), 2) * inv_n
        yc = y - mean
        var = _group_allreduce(jnp.sum(yc * yc, axis=1, keepdims=True), 2) * inv_n
        yn = yc * lax.rsqrt(var + GN_EPS)
        out = (yn * gain_ref[...] + bias_ref[...] + bonus_ref[b]) * g_ref[b]
        o_ref[b] = out.astype(o_ref.dtype)


def rwkv_scan(r, w, k, v, a, b, bonus, g, gn_gain, gn_bias, batch, tc=32):
    t_total = r.shape[0]
    s = t_total // batch
    tc = min(tc, s)
    packed = lambda x: x.reshape(batch, s, PACK_ROWS, LANES)
    spec = pl.BlockSpec((batch, tc, PACK_ROWS, LANES), lambda i: (0, i, 0, 0))
    affine = pl.BlockSpec((PACK_ROWS, LANES), lambda i: (0, 0))
    blk = 8 * _nbytes((batch, tc, PACK_ROWS, LANES), F32) + _nbytes((batch, tc, PACK_ROWS, LANES), BF16)
    state_shape = (batch, LANE_GROUPS, PACK_ROWS, PACK_ROWS, LANES)
    rep_shape = (5, batch, tc, LANE_GROUPS, PACK_ROWS, LANES)
    y_shape = (batch, tc, PACK_ROWS, LANES)
    scratch = _nbytes(state_shape, F32) + _nbytes(rep_shape, F32) + _nbytes(y_shape, F32)
    out = pl.pallas_call(
        functools.partial(_rwkv_scan_kernel, tc=tc, nb=batch),
        grid=(s // tc,),
        in_specs=[spec] * 8 + [affine, affine],
        out_specs=spec,
        out_shape=jax.ShapeDtypeStruct((batch, s, PACK_ROWS, LANES), BF16),
        scratch_shapes=[pltpu.VMEM(state_shape, F32), pltpu.VMEM(rep_shape, F32), pltpu.VMEM(y_shape, F32)],
        compiler_params=pltpu.CompilerParams(
            dimension_semantics=("arbitrary",),
            vmem_limit_bytes=_vmem_limit(blk, scratch_bytes=scratch, temp_bytes=8 * _nbytes(y_shape, F32))),
        name="rwkv_scan",
    )(packed(r), packed(w), packed(k), packed(v), packed(a), packed(b), packed(bonus), packed(g),
      gn_gain.reshape(PACK_ROWS, LANES), gn_bias.reshape(PACK_ROWS, LANES))
    return out.reshape(t_total, RWKV_WIDTH)


def _mem_attn_kernel(q_ref, k_ref, v_ref, qg_ref, kg_ref, o_ref):
    scale = MEM_HEAD_DIM ** -0.5
    for h in range(MEM_HEADS):
        cols = slice(h * MEM_HEAD_DIM, (h + 1) * MEM_HEAD_DIM)
        q = q_ref[:, cols]
        q = q * lax.rsqrt(jnp.mean(q * q, axis=-1, keepdims=True) + NORM_EPS) * qg_ref[...]
        k = k_ref[:, cols]
        k = k * lax.rsqrt(jnp.mean(k * k, axis=-1, keepdims=True) + NORM_EPS) * kg_ref[...]
        s = lax.dot_general(q.astype(BF16), k.astype(BF16), (((1,), (1,)), ((), ())),
                            preferred_element_type=F32) * scale
        p = jnp.exp(s - jnp.max(s, axis=-1, keepdims=True))
        p = p / jnp.sum(p, axis=-1, keepdims=True)
        o = jnp.dot(p.astype(BF16), v_ref[:, cols].astype(BF16), preferred_element_type=F32)
        o_ref[:, cols] = o.astype(o_ref.dtype)


def mem_attention(q3, k3, v3, q_gain, k_gain, tq=512):
    b, s, _ = q3.shape
    m = k3.shape[1]
    tq = min(tq, s)
    blk = _nbytes((tq, MEM_WIDTH), F32) + 2 * _nbytes((m, MEM_WIDTH), F32) + _nbytes((tq, MEM_WIDTH), BF16)
    return pl.pallas_call(
        _mem_attn_kernel,
        grid=(b, s // tq),
        in_specs=[pl.BlockSpec((None, tq, MEM_WIDTH), lambda bi, i: (bi, i, 0)),
                  pl.BlockSpec((None, m, MEM_WIDTH), lambda bi, i: (bi, 0, 0)),
                  pl.BlockSpec((None, m, MEM_WIDTH), lambda bi, i: (bi, 0, 0)),
                  pl.BlockSpec((1, MEM_HEAD_DIM), lambda bi, i: (0, 0)),
                  pl.BlockSpec((1, MEM_HEAD_DIM), lambda bi, i: (0, 0))],
        out_specs=pl.BlockSpec((None, tq, MEM_WIDTH), lambda bi, i: (bi, i, 0)),
        out_shape=jax.ShapeDtypeStruct((b, s, MEM_WIDTH), BF16),
        compiler_params=pltpu.CompilerParams(
            dimension_semantics=("parallel", "parallel"),
            vmem_limit_bytes=_vmem_limit(blk, temp_bytes=8 * _nbytes((tq, m), F32))),
        name="mem_attention",
    )(q3, k3, v3, q_gain.reshape(1, -1), k_gain.reshape(1, -1))


def _projection_tail_kernel(w_ref, o_ref, *, tn, valid):
    row = pl.program_id(0) * tn + lax.broadcasted_iota(jnp.int32, w_ref.shape, 0)
    o_ref[...] = jnp.where(row < valid, w_ref[...], 0.0).astype(o_ref.dtype)


def projection_tail(w_in_t, layer, tn=512):
    _, total, kdim = w_in_t.shape
    return pl.pallas_call(
        functools.partial(_projection_tail_kernel, tn=tn, valid=total - FOX_QKVG),
        grid=(RWKV_PAD // tn,),
        in_specs=[pl.BlockSpec((None, tn, kdim), lambda j: (layer, FOX_QKVG // tn + j, 0))],
        out_specs=pl.BlockSpec((tn, kdim), lambda j: (j, 0)),
        out_shape=jax.ShapeDtypeStruct((RWKV_PAD, kdim), BF16),
        compiler_params=pltpu.CompilerParams(
            dimension_semantics=("parallel",),
            vmem_limit_bytes=_vmem_limit(_nbytes((tn, kdim), F32) + _nbytes((tn, kdim), BF16),
                                         temp_bytes=_nbytes((tn, kdim), F32))),
        name="projection_tail",
    )(w_in_t)


def _rwkv_projection_weight(w_in_t, layer):
    tail = projection_tail(w_in_t, layer)

    def rows(lo, hi):
        return tail[FOX_HEADS + lo:FOX_HEADS + hi]

    parts = [_to_column_layout(rows(R_OFF, WLO_OFF), axis=0), rows(WLO_OFF, K_OFF),
             _to_column_layout(rows(K_OFF, V_OFF), axis=0), _to_value_layout(rows(V_OFF, ALO_OFF), axis=0),
             rows(ALO_OFF, RWKV_IN), jnp.zeros((RWKV_PAD - RWKV_IN, w_in_t.shape[2]), BF16)]
    return jnp.concatenate(parts, axis=0)


def _rwkv_shift_mix(mu):
    parts = [_to_column_layout(mu[R_OFF:WLO_OFF]), mu[WLO_OFF:K_OFF], _to_column_layout(mu[K_OFF:V_OFF]),
             _to_value_layout(mu[V_OFF:ALO_OFF]), mu[ALO_OFF:], jnp.zeros((RWKV_READ - RWKV_IN,), mu.dtype)]
    return jnp.concatenate(parts)


def _layer(x, mem2, b, s, layer, p, w):
    t_total = b * s
    h = rmsnorm_bf16(x, p["norm_mix"])
    w_in_t = jnp.swapaxes(w["w_in"], 1, 2)
    pf = matmul(h, w_in_t, layer=layer, n=FOX_QKVG, out_dtype=BF16, tm=2048, single_buffer_a=True,
                w_transposed=True)
    pr = matmul(h, _rwkv_projection_weight(w_in_t, layer), tm=2048, single_buffer_a=True, w_transposed=True)

    pf3 = pf.reshape(b, s, FOX_QKVG)
    qn, kn, c = fox_prep(pf3, h.reshape(b, s, D_MODEL), w_in_t, layer, p["fox_f_bias"],
                         p["fox_q_gain"], p["fox_k_gain"])
    y_fox = fox_attention(qn, kn, c, pf3).reshape(t_total, FOX_WIDTH)

    g2p = _to_value_layout(jnp.pad(p["rwkv_g2"], ((0, GATE_PAD - GATE_LORA), (0, 0)))).astype(BF16)
    r, wd, k, v, a, bb, g, bonus = rwkv_prep(
        pr, s, _rwkv_shift_mix(p["rwkv_mu"]), _to_column_layout(p["rwkv_w0"]), _to_column_layout(p["rwkv_w2"]),
        _to_column_layout(p["rwkv_a0"]), _to_column_layout(p["rwkv_a2"]), g2p, _to_column_layout(p["rwkv_k_k"]),
        _to_column_layout(p["rwkv_k_a"]), _to_column_layout(p["rwkv_r_k"].reshape(-1)))
    y_rwkv = rwkv_scan(r, wd, k, v, a, bb, bonus, g, _to_value_layout(p["rwkv_gn_gain"]),
                       _to_value_layout(p["rwkv_gn_bias"]), b)

    w_rwkv_rows = _to_value_layout(w["w_out"][layer, FOX_WIDTH:, :], axis=0)
    x = out_proj(y_fox, y_rwkv, w["w_out"], layer, w_rwkv_rows, x)

    h = rmsnorm_bf16(x, p["norm_mem_q"])
    m = rmsnorm_bf16(mem2, p["norm_mem_kv"])
    q = matmul(h, w["mem_w_q"], layer=layer)
    km = matmul(m, w["mem_w_k"], layer=layer)
    vm = matmul(m, w["mem_w_v"], layer=layer)
    n_mem = mem2.shape[0] // b
    o = mem_attention(q.reshape(b, s, MEM_WIDTH), km.reshape(b, n_mem, MEM_WIDTH),
                      vm.reshape(b, n_mem, MEM_WIDTH), p["mem_q_gain"], p["mem_k_gain"])
    x = matmul(o.reshape(t_total, MEM_WIDTH), w["mem_w_o"], layer=layer, epilogue="residual", residual=x,
               tn=1024)

    h = rmsnorm_bf16(x, p["norm_mlp"])
    u = matmul(h, w["w_up"], layer=layer, epilogue="relu2", out_dtype=BF16, tm=2048, single_buffer_a=True)
    x = matmul(u, w["w_down"], layer=layer, epilogue="residual", residual=x, tm=1024, tn=1024, tk=2048)
    return x


_PARAM_NAMES = ("norm_mix", "w_in", "fox_q_gain", "fox_k_gain", "fox_f_bias", "rwkv_mu", "rwkv_w0", "rwkv_w2",
                "rwkv_a0", "rwkv_a2", "rwkv_g2", "rwkv_k_k", "rwkv_k_a", "rwkv_r_k", "rwkv_gn_gain",
                "rwkv_gn_bias", "w_out", "norm_mem_q", "norm_mem_kv", "mem_w_q", "mem_w_k", "mem_w_v",
                "mem_q_gain", "mem_k_gain", "mem_w_o", "norm_mlp", "w_up", "w_down")
_STACKED_WEIGHTS = ("w_in", "w_out", "mem_w_q", "mem_w_k", "mem_w_v", "mem_w_o", "w_up", "w_down")


def kernel(x, mem, norm_mix, w_in, fox_q_gain, fox_k_gain, fox_f_bias, rwkv_mu, rwkv_w0, rwkv_w2, rwkv_a0, rwkv_a2, rwkv_g2, rwkv_k_k, rwkv_k_a, rwkv_r_k, rwkv_gn_gain, rwkv_gn_bias, w_out, norm_mem_q, norm_mem_kv, mem_w_q, mem_w_k, mem_w_v, mem_q_gain, mem_k_gain, mem_w_o, norm_mlp, w_up, w_down):
    params = dict(zip(_PARAM_NAMES, (norm_mix, w_in, fox_q_gain, fox_k_gain, fox_f_bias, rwkv_mu, rwkv_w0,
                                     rwkv_w2, rwkv_a0, rwkv_a2, rwkv_g2, rwkv_k_k, rwkv_k_a, rwkv_r_k,
                                     rwkv_gn_gain, rwkv_gn_bias, w_out, norm_mem_q, norm_mem_kv, mem_w_q,
                                     mem_w_k, mem_w_v, mem_q_gain, mem_k_gain, mem_w_o, norm_mlp, w_up, w_down)))
    b, s, d = x.shape
    assert d == D_MODEL, x.shape
    depth = w_in.shape[0]
    stacked = {name: params[name] for name in _STACKED_WEIGHTS}
    x2 = x.reshape(b * s, d)
    mem2 = mem.reshape(-1, d)
    for layer in range(depth):
        small = {name: value[layer] for name, value in params.items() if name not in _STACKED_WEIGHTS}
        x2 = _layer(x2, mem2, b, s, layer, small, stacked)
    return x2.reshape(b, s, d)
```

```python
import functools

import jax
import jax.numpy as jnp
from jax import lax
from jax.experimental import pallas as pl
from jax.experimental.pallas import tpu as pltpu

D_MODEL = 4096
FOX_WIDTH = 2048
FOX_HEAD_DIM = 128
FOX_HEADS = 16
RWKV_WIDTH = 2048
RWKV_HEAD_DIM = 64
RWKV_HEADS = 32
DECAY_LORA = 128
AAA_LORA = 128
GATE_LORA = 480
MEM_HEADS = 4
MEM_HEAD_DIM = 128
MEM_WIDTH = MEM_HEADS * MEM_HEAD_DIM
NORM_EPS = 1e-6
GN_EPS = 64e-5
FOX_QKVG = 4 * FOX_WIDTH
FOX_IN = FOX_QKVG + FOX_HEADS
RWKV_IN = 3 * RWKV_WIDTH + DECAY_LORA + AAA_LORA + GATE_LORA
R_OFF, WLO_OFF, K_OFF, V_OFF, ALO_OFF, GLO_OFF = 0, 2048, 2176, 4224, 6272, 6400

LANES = 128
SUBLANES = 8
VMEM_BYTES_V7X = 64 * 1024 * 1024
RWKV_READ = 6912
RWKV_PAD = 7168
GATE_PAD = RWKV_READ - GLO_OFF

LANE_GROUPS = LANES // RWKV_HEADS
PACK_ROWS = RWKV_WIDTH // LANES

F32 = jnp.float32
BF16 = jnp.bfloat16
HIGHEST = lax.Precision.HIGHEST
LOG2_E = 1.4426950408889634


def _vmem_limit(block_bytes, scratch_bytes=0, temp_bytes=0):
    need = 2 * block_bytes + scratch_bytes + temp_bytes + (4 << 20)
    return int(min(need, VMEM_BYTES_V7X - (6 << 20)))


def _nbytes(shape, dtype):
    n = 1
    for s in shape:
        n *= s
    return n * jnp.dtype(dtype).itemsize


def _rmsnorm_kernel(x_ref, g_ref, o_ref):
    x = x_ref[...]
    ms = jnp.mean(x * x, axis=-1, keepdims=True)
    o_ref[...] = (x * lax.rsqrt(ms + NORM_EPS) * g_ref[...]).astype(o_ref.dtype)


def rmsnorm_bf16(x, gain, tm=256):
    m, d = x.shape
    tm = min(tm, m)
    return pl.pallas_call(
        _rmsnorm_kernel,
        grid=(m // tm,),
        in_specs=[pl.BlockSpec((tm, d), lambda i: (i, 0)),
                  pl.BlockSpec((1, d), lambda i: (0, 0))],
        out_specs=pl.BlockSpec((tm, d), lambda i: (i, 0)),
        out_shape=jax.ShapeDtypeStruct((m, d), BF16),
        compiler_params=pltpu.CompilerParams(
            dimension_semantics=("parallel",),
            vmem_limit_bytes=_vmem_limit(_nbytes((tm, d), F32) + _nbytes((tm, d), BF16),
                                         temp_bytes=2 * _nbytes((tm, d), F32))),
        name="rmsnorm",
    )(x, gain.reshape(1, d))


def _matmul_kernel(*refs, epilogue, nk, w_transposed):
    if epilogue == "residual":
        a_ref, w_ref, r_ref, o_ref = refs
    else:
        a_ref, w_ref, o_ref = refs

    def product():
        w = w_ref[...].astype(BF16)
        if w_transposed:
            return lax.dot_general(a_ref[...], w, (((1,), (1,)), ((), ())), preferred_element_type=F32)
        return jnp.dot(a_ref[...], w, preferred_element_type=F32)

    if nk == 1:
        d = product()
        if epilogue == "relu2":
            r = jnp.maximum(d, 0.0)
            o_ref[...] = (r * r).astype(o_ref.dtype)
        elif epilogue == "residual":
            o_ref[...] = r_ref[...] + d
        else:
            o_ref[...] = d.astype(o_ref.dtype)
    else:
        k = pl.program_id(2)

        @pl.when(k == 0)
        def _():
            o_ref[...] = r_ref[...] + product() if epilogue == "residual" else product()

        @pl.when(k > 0)
        def _():
            o_ref[...] += product()


def matmul(a, w, *, layer=None, n=None, epilogue="none", residual=None, out_dtype=F32,
           tm=1024, tn=512, tk=None, single_buffer_a=False, w_transposed=False):
    m, kdim = a.shape
    n = w.shape[-2 if w_transposed else -1] if n is None else n
    tk = kdim if tk is None else tk
    tm, tn = min(tm, m), min(tn, n)
    assert m % tm == 0 and n % tn == 0 and kdim % tk == 0, (a.shape, w.shape, tm, tn, tk)
    nk = kdim // tk
    assert nk == 1 or (out_dtype == F32 and epilogue != "relu2")
    w_block = (tn, tk) if w_transposed else (tk, tn)
    w_index = (lambda k, j: (j, k)) if w_transposed else (lambda k, j: (k, j))
    if layer is None:
        w_spec = pl.BlockSpec(w_block, lambda i, j, k: w_index(k, j))
    else:
        w_spec = pl.BlockSpec((None,) + w_block, lambda i, j, k: (layer,) + w_index(k, j))
    a_mode = pl.Buffered(1) if single_buffer_a else None
    in_specs = [pl.BlockSpec((tm, tk), lambda i, j, k: (i, k), pipeline_mode=a_mode), w_spec]
    args = [a, w]
    a_bytes = _nbytes((tm, tk), BF16)
    blk = (a_bytes // 2 if single_buffer_a else a_bytes) + _nbytes((tk, tn), w.dtype) + _nbytes((tm, tn), out_dtype)
    if epilogue == "residual":
        in_specs.append(pl.BlockSpec((tm, tn), lambda i, j, k: (i, j)))
        args.append(residual)
        blk += _nbytes((tm, tn), F32)
    return pl.pallas_call(
        functools.partial(_matmul_kernel, epilogue=epilogue, nk=nk, w_transposed=w_transposed),
        grid=(m // tm, n // tn, nk),
        in_specs=in_specs,
        out_specs=pl.BlockSpec((tm, tn), lambda i, j, k: (i, j)),
        out_shape=jax.ShapeDtypeStruct((m, n), out_dtype),
        compiler_params=pltpu.CompilerParams(
            dimension_semantics=("parallel", "parallel", "arbitrary"),
            vmem_limit_bytes=_vmem_limit(blk, temp_bytes=_nbytes((tk, tn), BF16) + 2 * _nbytes((tm, tn), F32))),
        name="matmul_" + epilogue,
    )(*args)


def _out_proj_kernel(a1_ref, a2_ref, w1_ref, w2_ref, r_ref, o_ref):
    d = jnp.dot(a1_ref[...], w1_ref[...].astype(BF16), preferred_element_type=F32)
    d = d + jnp.dot(a2_ref[...], w2_ref[...].astype(BF16), preferred_element_type=F32)
    o_ref[...] = r_ref[...] + d


def out_proj(y_fox, y_rwkv, w_out, layer, w_rwkv_rows, x, tm=1024, tn=512):
    m, half = y_fox.shape
    n = x.shape[1]
    tm = min(tm, m)
    assert m % tm == 0 and n % tn == 0, (m, n, tm, tn)
    blk = 2 * _nbytes((tm, half), BF16) + 2 * _nbytes((half, tn), F32) + 2 * _nbytes((tm, tn), F32)
    return pl.pallas_call(
        _out_proj_kernel,
        grid=(m // tm, n // tn),
        in_specs=[pl.BlockSpec((tm, half), lambda i, j: (i, 0)),
                  pl.BlockSpec((tm, half), lambda i, j: (i, 0)),
                  pl.BlockSpec((None, half, tn), lambda i, j: (layer, 0, j)),
                  pl.BlockSpec((half, tn), lambda i, j: (0, j)),
                  pl.BlockSpec((tm, tn), lambda i, j: (i, j))],
        out_specs=pl.BlockSpec((tm, tn), lambda i, j: (i, j)),
        out_shape=jax.ShapeDtypeStruct((m, n), F32),
        compiler_params=pltpu.CompilerParams(
            dimension_semantics=("parallel", "parallel"),
            vmem_limit_bytes=_vmem_limit(blk, temp_bytes=2 * _nbytes((half, tn), BF16) + 2 * _nbytes((tm, tn), F32))),
        name="out_proj",
    )(y_fox, y_rwkv, w_out, w_rwkv_rows, x)


def _fox_prep_kernel(p_ref, h_ref, wf_ref, fb_ref, qg_ref, kg_ref, tri_ref,
                     q_o, k_o, c_o, carry_ref, *, tq):
    @pl.when(pl.program_id(1) == 0)
    def _():
        carry_ref[...] = jnp.zeros_like(carry_ref)

    scale = FOX_HEAD_DIM ** -0.5 * LOG2_E
    for h in range(FOX_HEADS):
        lo = h * FOX_HEAD_DIM
        q = p_ref[:, lo:lo + FOX_HEAD_DIM].astype(F32)
        ms = jnp.mean(q * q, axis=-1, keepdims=True)
        q_o[:, lo:lo + FOX_HEAD_DIM] = (q * lax.rsqrt(ms + NORM_EPS) * qg_ref[...] * scale).astype(BF16)
        k = p_ref[:, FOX_WIDTH + lo:FOX_WIDTH + lo + FOX_HEAD_DIM].astype(F32)
        ms = jnp.mean(k * k, axis=-1, keepdims=True)
        k_o[:, lo:lo + FOX_HEAD_DIM] = (k * lax.rsqrt(ms + NORM_EPS) * kg_ref[...]).astype(BF16)

    f_logit = lax.dot_general(h_ref[...], wf_ref[...].astype(BF16), (((1,), (1,)), ((), ())),
                              preferred_element_type=F32)
    log_f = jax.nn.log_sigmoid(f_logit + fb_ref[...])
    c = jnp.dot(tri_ref[...], log_f, precision=HIGHEST, preferred_element_type=F32) + carry_ref[0:1, :]
    c_o[...] = c * LOG2_E
    carry_ref[...] = jnp.broadcast_to(c[tq - 1:tq, :], carry_ref.shape)


def fox_prep(pf3, h3, w_in_t, layer, f_bias, q_gain, k_gain, tq=512):
    b, s, _ = pf3.shape
    tq = min(tq, s)
    tri = (lax.broadcasted_iota(jnp.int32, (tq, tq), 0) >= lax.broadcasted_iota(jnp.int32, (tq, tq), 1)).astype(F32)
    qk_w = 2 * FOX_WIDTH
    act = jax.ShapeDtypeStruct((b, s, FOX_WIDTH), BF16)
    fb = jnp.pad(f_bias, (0, LANES - FOX_HEADS)).reshape(1, LANES)
    blk = (_nbytes((tq, qk_w), BF16) + _nbytes((tq, D_MODEL), BF16) + _nbytes((LANES, D_MODEL), F32)
           + _nbytes((tq, tq), F32) + 2 * _nbytes((tq, FOX_WIDTH), BF16) + _nbytes((tq, LANES), F32))
    qn, kn, c = pl.pallas_call(
        functools.partial(_fox_prep_kernel, tq=tq),
        grid=(b, s // tq),
        in_specs=[pl.BlockSpec((None, tq, qk_w), lambda bi, i: (bi, i, 0)),
                  pl.BlockSpec((None, tq, D_MODEL), lambda bi, i: (bi, i, 0)),
                  pl.BlockSpec((None, LANES, D_MODEL), lambda bi, i: (layer, FOX_QKVG // LANES, 0)),
                  pl.BlockSpec((1, LANES), lambda bi, i: (0, 0)),
                  pl.BlockSpec((1, FOX_HEAD_DIM), lambda bi, i: (0, 0)),
                  pl.BlockSpec((1, FOX_HEAD_DIM), lambda bi, i: (0, 0)),
                  pl.BlockSpec((tq, tq), lambda bi, i: (0, 0))],
        out_specs=[pl.BlockSpec((None, tq, FOX_WIDTH), lambda bi, i: (bi, i, 0)),
                   pl.BlockSpec((None, tq, FOX_WIDTH), lambda bi, i: (bi, i, 0)),
                   pl.BlockSpec((None, tq, LANES), lambda bi, i: (bi, i, 0))],
        out_shape=[act, act, jax.ShapeDtypeStruct((b, s, LANES), F32)],
        scratch_shapes=[pltpu.VMEM((SUBLANES, LANES), F32)],
        compiler_params=pltpu.CompilerParams(
            dimension_semantics=("parallel", "arbitrary"),
            vmem_limit_bytes=_vmem_limit(blk, temp_bytes=_nbytes((tq, qk_w), F32))),
        name="fox_prep",
    )(pf3, h3, w_in_t, fb, q_gain.reshape(1, -1), k_gain.reshape(1, -1), tri)
    return qn, kn, c[:, :, :FOX_HEADS].transpose(0, 2, 1)


FOX_HEADS_PER_STEP = 1


def _fox_attn_kernel(q_ref, k_ref, v_ref, c_ref, gate_ref, o_ref, m_ref, l_ref, acc_ref, s_ref, p_ref, *, t):
    qi = pl.program_id(2)
    heads = range(FOX_HEADS_PER_STEP)
    m_ref[...] = jnp.full_like(m_ref, -jnp.inf)
    l_ref[...] = jnp.zeros_like(l_ref)
    acc_ref[...] = jnp.zeros_like(acc_ref)
    for hh in heads:
        p_ref[hh, 1] = jnp.zeros((t, t), BF16)

    def cols(hh):
        return slice(hh * FOX_HEAD_DIM, (hh + 1) * FOX_HEAD_DIM)

    def rows(ref, blk, hh):
        return ref[pl.ds(pl.multiple_of(blk * t, t), t), cols(hh)]

    c_base = [c_ref[hh, pl.ds(qi, 1), :][:, 0:1] for hh in heads]

    def scores(ki, hh):
        s = lax.dot_general(q_ref[:, cols(hh)], rows(k_ref, ki, hh), (((1,), (1,)), ((), ())),
                            preferred_element_type=F32)
        return s + (c_base[hh] - c_ref[hh, pl.ds(ki, 1), :])

    def weighted_values(slot, blk, hh):
        return jnp.dot(p_ref[hh, slot], rows(v_ref, blk, hh), preferred_element_type=F32)

    def step(ki, masked):
        for hh in heads:
            s = s_ref[hh, ki % 2]
            pv_prev = weighted_values((ki + 1) % 2, jnp.maximum(ki - 1, 0), hh)
            if not masked:
                s_ref[hh, (ki + 1) % 2] = scores(ki + 1, hh)
            else:
                row = lax.broadcasted_iota(jnp.int32, (t, t), 0)
                col = lax.broadcasted_iota(jnp.int32, (t, t), 1)
                s = jnp.where(col <= row, s, -jnp.inf)
            m_prev = m_ref[hh]
            m_new = jnp.maximum(m_prev, jnp.max(s, axis=1, keepdims=True))
            alpha = jnp.exp2(m_prev - m_new)
            p = jnp.exp2(s - jnp.tile(m_new, (1, t // LANES)))
            l_ref[hh] = alpha * l_ref[hh] + jnp.sum(p, axis=1, keepdims=True)
            acc_ref[hh] = alpha * (acc_ref[hh] + pv_prev)
            m_ref[hh] = m_new
            p_ref[hh, ki % 2] = p.astype(BF16)

    for hh in heads:
        s_ref[hh, 0] = scores(0, hh)

    def body(ki, carry):
        step(ki, False)
        return carry

    lax.fori_loop(0, qi, body, 0)
    step(qi, True)
    for hh in heads:
        o = (acc_ref[hh] + weighted_values(qi % 2, qi, hh)) / l_ref[hh]
        o_ref[:, cols(hh)] = (o * jax.nn.sigmoid(gate_ref[:, cols(hh)].astype(F32))).astype(o_ref.dtype)


def fox_attention(qn, kn, c, pf3, t=512):
    b, s, _ = qn.shape
    t = min(t, s)
    nt = s // t
    hp = FOX_HEADS_PER_STEP
    width = hp * FOX_HEAD_DIM
    c4 = c.reshape(b, FOX_HEADS, nt, t)
    value_blk0 = 2 * FOX_WIDTH // width
    gate_blk0 = 3 * FOX_WIDTH // width
    blk = 3 * _nbytes((t, width), BF16) + 2 * _nbytes((s, width), BF16) + hp * _nbytes((nt, t), F32)
    scratch = hp * (3 * _nbytes((t, LANES), F32) + 3 * _nbytes((t, t), F32))
    return pl.pallas_call(
        functools.partial(_fox_attn_kernel, t=t),
        grid=(b, FOX_HEADS // hp, nt),
        in_specs=[pl.BlockSpec((None, t, width), lambda bi, h, qi: (bi, qi, h)),
                  pl.BlockSpec((None, s, width), lambda bi, h, qi: (bi, 0, h)),
                  pl.BlockSpec((None, s, width), lambda bi, h, qi: (bi, 0, value_blk0 + h)),
                  pl.BlockSpec((None, hp, nt, t), lambda bi, h, qi: (bi, h, 0, 0)),
                  pl.BlockSpec((None, t, width), lambda bi, h, qi: (bi, qi, gate_blk0 + h))],
        out_specs=pl.BlockSpec((None, t, width), lambda bi, h, qi: (bi, qi, h)),
        out_shape=jax.ShapeDtypeStruct((b, s, FOX_WIDTH), BF16),
        scratch_shapes=[pltpu.VMEM((hp, t, LANES), F32), pltpu.VMEM((hp, t, LANES), F32),
                        pltpu.VMEM((hp, t, FOX_HEAD_DIM), F32),
                        pltpu.VMEM((hp, 2, t, t), F32), pltpu.VMEM((hp, 2, t, t), BF16)],
        compiler_params=pltpu.CompilerParams(
            dimension_semantics=("parallel", "parallel", "arbitrary"),
            vmem_limit_bytes=_vmem_limit(blk, scratch_bytes=scratch, temp_bytes=8 * _nbytes((t, t), F32))),
        name="fox_attention",
    )(qn, kn, pf3, c4, pf3)


def _to_column_layout(x, axis=-1):
    x = jnp.moveaxis(x, axis, -1)
    y = x.reshape(x.shape[:-1] + (RWKV_HEADS, RWKV_HEAD_DIM)).swapaxes(-1, -2).reshape(x.shape)
    return jnp.moveaxis(y, -1, axis)


def _to_value_layout(x, axis=-1):
    x = jnp.moveaxis(x, axis, -1)
    lead = x.ndim - 1
    y = x.reshape(x.shape[:-1] + (RWKV_HEADS, 2, LANE_GROUPS, SUBLANES))
    y = y.transpose(tuple(range(lead)) + (lead + 1, lead + 3, lead + 2, lead)).reshape(x.shape)
    return jnp.moveaxis(y, -1, axis)


def _group_allreduce(x, axis):
    x = x + pltpu.roll(x, 2 * RWKV_HEADS, axis=axis)
    return x + pltpu.roll(x, RWKV_HEADS, axis=axis)


def _head_sum(x):
    s = x[:, 0:LANES]
    for row in range(1, PACK_ROWS):
        s = s + x[:, row * LANES:(row + 1) * LANES]
    return _group_allreduce(s, 1)


def _rwkv_prep_kernel(p_ref, pprev_ref, mu_ref, w0_ref, w2_ref, a0_ref, a2_ref, g2_ref, kk_ref, ka_ref,
                      rk_ref, r_o, w_o, k_o, v_o, a_o, b_o, g_o, bonus_o, *, tiles_per_seq):
    first = (pl.program_id(0) % tiles_per_seq) == 0

    def shifted(lo, hi):
        p = p_ref[:, lo:hi]
        prev_row = jnp.where(first, 0.0, pprev_ref[SUBLANES - 1:SUBLANES, lo:hi])
        row = lax.broadcasted_iota(jnp.int32, p.shape, 0)
        prev = jnp.where(row == 0, prev_row, pltpu.roll(p, 1, axis=0))
        return p + (prev - p) * mu_ref[:, lo:hi]

    def over_rows(s):
        return jnp.tile(s, (1, PACK_ROWS))

    w_lo = shifted(WLO_OFF, WLO_OFF + DECAY_LORA)
    z = w0_ref[...] + jnp.dot(jnp.tanh(w_lo), w2_ref[...], precision=HIGHEST, preferred_element_type=F32)
    softplus_neg = jnp.maximum(-z, 0.0) + jnp.log1p(jnp.exp(-jnp.abs(z)))
    w_o[...] = jnp.exp(-jnp.exp(-softplus_neg - 0.5))

    a_lo = shifted(ALO_OFF, ALO_OFF + AAA_LORA)
    a_lr = jax.nn.sigmoid(a0_ref[...] + jnp.dot(a_lo, a2_ref[...], precision=HIGHEST,
                                                preferred_element_type=F32))

    g_lo = shifted(GLO_OFF, GLO_OFF + GATE_PAD)
    g_o[...] = jnp.dot(jax.nn.sigmoid(g_lo).astype(BF16), g2_ref[...], preferred_element_type=F32)

    k = shifted(K_OFF, K_OFF + RWKV_WIDTH)
    kk = k * kk_ref[...]
    kk = kk * over_rows(lax.rsqrt(jnp.maximum(_head_sum(kk * kk), 1e-24)))
    a_o[...] = -kk
    b_o[...] = kk * a_lr
    k = k * (1.0 + (a_lr - 1.0) * ka_ref[...])
    k_o[...] = k

    r = shifted(R_OFF, R_OFF + RWKV_WIDTH)
    r_o[...] = r
    v = shifted(V_OFF, V_OFF + RWKV_WIDTH)
    v_o[...] = v
    bonus_o[...] = over_rows(_head_sum(r * k * rk_ref[...])) * v


def rwkv_prep(pr, seq, mu, w0, w2, a0, a2, g2p, k_k, k_a, r_k, tq=128):
    t_total = pr.shape[0]
    width = RWKV_READ
    tq = min(tq, seq)
    row = lambda x: x.reshape(1, -1)
    full = lambda shape: pl.BlockSpec(shape, lambda i: (0, 0))
    out = jax.ShapeDtypeStruct((t_total, RWKV_WIDTH), F32)
    out_spec = pl.BlockSpec((tq, RWKV_WIDTH), lambda i: (i, 0))
    sub_per_tile = tq // SUBLANES
    blk = (_nbytes((tq, width), F32) + _nbytes((SUBLANES, width), F32) + 8 * _nbytes((tq, RWKV_WIDTH), F32)
           + _nbytes((DECAY_LORA + AAA_LORA, RWKV_WIDTH), F32) + _nbytes((GATE_PAD, RWKV_WIDTH), BF16))
    return pl.pallas_call(
        functools.partial(_rwkv_prep_kernel, tiles_per_seq=seq // tq),
        grid=(t_total // tq,),
        in_specs=[pl.BlockSpec((tq, width), lambda i: (i, 0)),
                  pl.BlockSpec((SUBLANES, width), lambda i: (jnp.maximum(i * sub_per_tile - 1, 0), 0)),
                  full((1, width)), full((1, RWKV_WIDTH)), full((DECAY_LORA, RWKV_WIDTH)),
                  full((1, RWKV_WIDTH)), full((AAA_LORA, RWKV_WIDTH)), full((GATE_PAD, RWKV_WIDTH)),
                  full((1, RWKV_WIDTH)), full((1, RWKV_WIDTH)), full((1, RWKV_WIDTH))],
        out_specs=[out_spec] * 8,
        out_shape=[out] * 8,
        compiler_params=pltpu.CompilerParams(
            dimension_semantics=("parallel",),
            vmem_limit_bytes=_vmem_limit(blk, temp_bytes=12 * _nbytes((tq, RWKV_WIDTH), F32))),
        name="rwkv_prep",
    )(pr, pr, row(mu), row(w0), w2, row(a0), a2, g2p, row(k_k), row(k_a), row(r_k))


V_TILES = PACK_ROWS // SUBLANES
ACCUMULATORS = 4


def _rwkv_scan_kernel(r_ref, w_ref, k_ref, a_ref, b_ref, rn_ref, wn_ref, kn_ref, an_ref, bn_ref,
                      v_ref, bonus_ref, g_ref, gain_ref, bias_ref, o_ref,
                      state_ref, rep_even_ref, rep_odd_ref, y_ref, *, tc, nb):
    lane_group = lax.broadcasted_iota(jnp.int32, (1, 1, LANES), 2) // RWKV_HEADS

    def replicate(x, lanes):
        rolled = [x] + [pltpu.roll(x, j * RWKV_HEADS, axis=x.ndim - 1) for j in range(1, LANE_GROUPS)]
        outs = []
        for q in range(LANE_GROUPS):
            out = rolled[(LANE_GROUPS - 1 - q) % LANE_GROUPS]
            for grp in range(LANE_GROUPS - 2, -1, -1):
                out = jnp.where(lanes == grp, rolled[(grp - q) % LANE_GROUPS], out)
            outs.append(out)
        return outs

    halves = [slice(half * SUBLANES, (half + 1) * SUBLANES) for half in range(PACK_ROWS // SUBLANES)]

    current = (r_ref, w_ref, k_ref, a_ref, b_ref)
    upcoming = (rn_ref, wn_ref, kn_ref, an_ref, bn_ref)

    @pl.when(pl.program_id(0) == 0)
    def _():
        state_ref[...] = jnp.zeros_like(state_ref)
        for i, ref in enumerate(current):
            for b in range(nb):
                for rows in halves:
                    for q, out in enumerate(replicate(ref[b, 0:tc, rows, :], lane_group)):
                        rep_even_ref[i, b, :, q, rows, :] = out

    R, W, K, A, B = range(5)

    def tile(g):
        return pl.ds(g * SUBLANES, SUBLANES)

    def total(parts):
        while len(parts) > 1:
            parts = [parts[i] + parts[i + 1] for i in range(0, len(parts), 2)]
        return parts[0]

    columns = [(q, row) for q in range(LANE_GROUPS) for row in range(PACK_ROWS)]

    def run_chunk(base, rep_ref, fill_ref, fill_src, fill_base):
        def bcast(i, b, t, q, row):
            return jnp.broadcast_to(rep_ref[i, b, t, q, row:row + 1, :], (SUBLANES, LANES))

        def step(t, carry):
            for i, ref in enumerate(fill_src):
                for b in range(nb):
                    for rows in halves:
                        for q, out in enumerate(replicate(ref[b, fill_base + t, rows, :], lane_group[0])):
                            fill_ref[i, b, t, q, rows, :] = out
            for b in range(nb):
                sa = [[None] * ACCUMULATORS for _ in range(V_TILES)]
                for idx, (q, row) in enumerate(columns):
                    ab = bcast(A, b, t, q, row)
                    for g in range(V_TILES):
                        term = state_ref[b, q, row, tile(g), :] * ab
                        slot = idx % ACCUMULATORS
                        sa[g][slot] = term if sa[g][slot] is None else sa[g][slot] + term
                sa = [total(parts) for parts in sa]
                vt = [v_ref[b, base + t, tile(g), :] for g in range(V_TILES)]
                y = [[None] * ACCUMULATORS for _ in range(V_TILES)]
                for idx, (q, row) in enumerate(columns):
                    wb, bb = bcast(W, b, t, q, row), bcast(B, b, t, q, row)
                    kb, rb = bcast(K, b, t, q, row), bcast(R, b, t, q, row)
                    for g in range(V_TILES):
                        s = state_ref[b, q, row, tile(g), :] * wb + sa[g] * bb + vt[g] * kb
                        state_ref[b, q, row, tile(g), :] = s
                        slot = idx % ACCUMULATORS
                        y[g][slot] = s * rb if y[g][slot] is None else y[g][slot] + s * rb
                for g in range(V_TILES):
                    y_ref[b, base + t, tile(g), :] = total(y[g])
            return carry

        lax.fori_loop(0, tc, step, 0)

    run_chunk(0, rep_even_ref, rep_odd_ref, current, tc)
    run_chunk(tc, rep_odd_ref, rep_even_ref, upcoming, 0)

    inv_n = 1.0 / RWKV_HEAD_DIM
    for b in range(nb):
        y = y_ref[b]
        mean = _group_allreduce(jnp.sum(y, axis=1, keepdims=True), 2) * inv_n
        yc = y - mean
        var = _group_allreduce(jnp.sum(yc * yc, axis=1, keepdims=True), 2) * inv_n
        yn = yc * lax.rsqrt(var + GN_EPS)
        out = (yn * gain_ref[...] + bias_ref[...] + bonus_ref[b]) * g_ref[b]
        o_ref[b] = out.astype(o_ref.dtype)


def rwkv_scan(r, w, k, v, a, b, bonus, g, gn_gain, gn_bias, batch, tc=32):
    t_total = r.shape[0]
    s = t_total // batch
    tc = min(tc, s // 2)
    n_steps = s // (2 * tc)
    assert s == n_steps * 2 * tc, (s, tc)
    packed = lambda x: x.reshape(batch, s, PACK_ROWS, LANES)
    spec = pl.BlockSpec((batch, 2 * tc, PACK_ROWS, LANES), lambda i: (0, i, 0, 0))
    next_spec = pl.BlockSpec((batch, tc, PACK_ROWS, LANES),
                             lambda i: (0, jnp.minimum(2 * i + 2, 2 * n_steps - 2), 0, 0))
    affine = pl.BlockSpec((PACK_ROWS, LANES), lambda i: (0, 0))
    chunk_bytes = _nbytes((batch, tc, PACK_ROWS, LANES), F32)
    blk = (2 * 8 + 5 + 1) * chunk_bytes
    state_shape = (batch, LANE_GROUPS, PACK_ROWS, PACK_ROWS, LANES)
    rep_shape = (5, batch, tc, LANE_GROUPS, PACK_ROWS, LANES)
    y_shape = (batch, 2 * tc, PACK_ROWS, LANES)
    scratch = _nbytes(state_shape, F32) + 2 * _nbytes(rep_shape, F32) + _nbytes(y_shape, F32)
    per_column = [packed(x) for x in (r, w, k, a, b)]
    out = pl.pallas_call(
        functools.partial(_rwkv_scan_kernel, tc=tc, nb=batch),
        grid=(n_steps,),
        in_specs=[spec] * 5 + [next_spec] * 5 + [spec] * 3 + [affine, affine],
        out_specs=spec,
        out_shape=jax.ShapeDtypeStruct((batch, s, PACK_ROWS, LANES), BF16),
        scratch_shapes=[pltpu.VMEM(state_shape, F32), pltpu.VMEM(rep_shape, F32), pltpu.VMEM(rep_shape, F32),
                        pltpu.VMEM(y_shape, F32)],
        compiler_params=pltpu.CompilerParams(
            dimension_semantics=("arbitrary",),
            vmem_limit_bytes=_vmem_limit(blk, scratch_bytes=scratch, temp_bytes=8 * _nbytes(y_shape, F32))),
        name="rwkv_scan",
    )(*per_column, *per_column, packed(v), packed(bonus), packed(g),
      gn_gain.reshape(PACK_ROWS, LANES), gn_bias.reshape(PACK_ROWS, LANES))
    return out.reshape(t_total, RWKV_WIDTH)


def _mem_attn_kernel(q_ref, k_ref, v_ref, qg_ref, kg_ref, o_ref):
    scale = MEM_HEAD_DIM ** -0.5
    for h in range(MEM_HEADS):
        cols = slice(h * MEM_HEAD_DIM, (h + 1) * MEM_HEAD_DIM)
        q = q_ref[:, cols]
        q = q * lax.rsqrt(jnp.mean(q * q, axis=-1, keepdims=True) + NORM_EPS) * qg_ref[...]
        k = k_ref[:, cols]
        k = k * lax.rsqrt(jnp.mean(k * k, axis=-1, keepdims=True) + NORM_EPS) * kg_ref[...]
        s = lax.dot_general(q.astype(BF16), k.astype(BF16), (((1,), (1,)), ((), ())),
                            preferred_element_type=F32) * scale
        p = jnp.exp(s - jnp.max(s, axis=-1, keepdims=True))
        p = p / jnp.sum(p, axis=-1, keepdims=True)
        o = jnp.dot(p.astype(BF16), v_ref[:, cols].astype(BF16), preferred_element_type=F32)
        o_ref[:, cols] = o.astype(o_ref.dtype)


def mem_attention(q3, k3, v3, q_gain, k_gain, tq=512):
    b, s, _ = q3.shape
    m = k3.shape[1]
    tq = min(tq, s)
    blk = _nbytes((tq, MEM_WIDTH), F32) + 2 * _nbytes((m, MEM_WIDTH), F32) + _nbytes((tq, MEM_WIDTH), BF16)
    return pl.pallas_call(
        _mem_attn_kernel,
        grid=(b, s // tq),
        in_specs=[pl.BlockSpec((None, tq, MEM_WIDTH), lambda bi, i: (bi, i, 0)),
                  pl.BlockSpec((None, m, MEM_WIDTH), lambda bi, i: (bi, 0, 0)),
                  pl.BlockSpec((None, m, MEM_WIDTH), lambda bi, i: (bi, 0, 0)),
                  pl.BlockSpec((1, MEM_HEAD_DIM), lambda bi, i: (0, 0)),
                  pl.BlockSpec((1, MEM_HEAD_DIM), lambda bi, i: (0, 0))],
        out_specs=pl.BlockSpec((None, tq, MEM_WIDTH), lambda bi, i: (bi, i, 0)),
        out_shape=jax.ShapeDtypeStruct((b, s, MEM_WIDTH), BF16),
        compiler_params=pltpu.CompilerParams(
            dimension_semantics=("parallel", "parallel"),
            vmem_limit_bytes=_vmem_limit(blk, temp_bytes=8 * _nbytes((tq, m), F32))),
        name="mem_attention",
    )(q3, k3, v3, q_gain.reshape(1, -1), k_gain.reshape(1, -1))


def _projection_tail_kernel(w_ref, o_ref, *, tn, valid):
    row = pl.program_id(0) * tn + lax.broadcasted_iota(jnp.int32, w_ref.shape, 0)
    o_ref[...] = jnp.where(row < valid, w_ref[...], 0.0).astype(o_ref.dtype)


def projection_tail(w_in_t, layer, tn=512):
    _, total, kdim = w_in_t.shape
    return pl.pallas_call(
        functools.partial(_projection_tail_kernel, tn=tn, valid=total - FOX_QKVG),
        grid=(RWKV_PAD // tn,),
        in_specs=[pl.BlockSpec((None, tn, kdim), lambda j: (layer, FOX_QKVG // tn + j, 0))],
        out_specs=pl.BlockSpec((tn, kdim), lambda j: (j, 0)),
        out_shape=jax.ShapeDtypeStruct((RWKV_PAD, kdim), BF16),
        compiler_params=pltpu.CompilerParams(
            dimension_semantics=("parallel",),
            vmem_limit_bytes=_vmem_limit(_nbytes((tn, kdim), F32) + _nbytes((tn, kdim), BF16),
                                         temp_bytes=_nbytes((tn, kdim), F32))),
        name="projection_tail",
    )(w_in_t)


def _rwkv_projection_weight(w_in_t, layer):
    tail = projection_tail(w_in_t, layer)

    def rows(lo, hi):
        return tail[FOX_HEADS + lo:FOX_HEADS + hi]

    parts = [_to_column_layout(rows(R_OFF, WLO_OFF), axis=0), rows(WLO_OFF, K_OFF),
             _to_column_layout(rows(K_OFF, V_OFF), axis=0), _to_value_layout(rows(V_OFF, ALO_OFF), axis=0),
             rows(ALO_OFF, RWKV_IN), jnp.zeros((RWKV_PAD - RWKV_IN, w_in_t.shape[2]), BF16)]
    return jnp.concatenate(parts, axis=0)


def _rwkv_shift_mix(mu):
    parts = [_to_column_layout(mu[R_OFF:WLO_OFF]), mu[WLO_OFF:K_OFF], _to_column_layout(mu[K_OFF:V_OFF]),
             _to_value_layout(mu[V_OFF:ALO_OFF]), mu[ALO_OFF:], jnp.zeros((RWKV_READ - RWKV_IN,), mu.dtype)]
    return jnp.concatenate(parts)


def _layer(x, mem2, b, s, layer, p, w):
    t_total = b * s
    h = rmsnorm_bf16(x, p["norm_mix"])
    w_in_t = jnp.swapaxes(w["w_in"], 1, 2)
    pf = matmul(h, w_in_t, layer=layer, n=FOX_QKVG, out_dtype=BF16, tm=2048, single_buffer_a=True,
                w_transposed=True)
    pr = matmul(h, _rwkv_projection_weight(w_in_t, layer), tm=2048, single_buffer_a=True, w_transposed=True)

    pf3 = pf.reshape(b, s, FOX_QKVG)
    qn, kn, c = fox_prep(pf3, h.reshape(b, s, D_MODEL), w_in_t, layer, p["fox_f_bias"],
                         p["fox_q_gain"], p["fox_k_gain"])
    y_fox = fox_attention(qn, kn, c, pf3).reshape(t_total, FOX_WIDTH)

    g2p = _to_value_layout(jnp.pad(p["rwkv_g2"], ((0, GATE_PAD - GATE_LORA), (0, 0)))).astype(BF16)
    r, wd, k, v, a, bb, g, bonus = rwkv_prep(
        pr, s, _rwkv_shift_mix(p["rwkv_mu"]), _to_column_layout(p["rwkv_w0"]), _to_column_layout(p["rwkv_w2"]),
        _to_column_layout(p["rwkv_a0"]), _to_column_layout(p["rwkv_a2"]), g2p, _to_column_layout(p["rwkv_k_k"]),
        _to_column_layout(p["rwkv_k_a"]), _to_column_layout(p["rwkv_r_k"].reshape(-1)))
    y_rwkv = rwkv_scan(r, wd, k, v, a, bb, bonus, g, _to_value_layout(p["rwkv_gn_gain"]),
                       _to_value_layout(p["rwkv_gn_bias"]), b)

    w_rwkv_rows = _to_value_layout(w["w_out"][layer, FOX_WIDTH:, :], axis=0)
    x = out_proj(y_fox, y_rwkv, w["w_out"], layer, w_rwkv_rows, x)

    h = rmsnorm_bf16(x, p["norm_mem_q"])
    m = rmsnorm_bf16(mem2, p["norm_mem_kv"])
    q = matmul(h, w["mem_w_q"], layer=layer)
    km = matmul(m, w["mem_w_k"], layer=layer)
    vm = matmul(m, w["mem_w_v"], layer=layer)
    n_mem = mem2.shape[0] // b
    o = mem_attention(q.reshape(b, s, MEM_WIDTH), km.reshape(b, n_mem, MEM_WIDTH),
                      vm.reshape(b, n_mem, MEM_WIDTH), p["mem_q_gain"], p["mem_k_gain"])
    x = matmul(o.reshape(t_total, MEM_WIDTH), w["mem_w_o"], layer=layer, epilogue="residual", residual=x,
               tn=1024)

    h = rmsnorm_bf16(x, p["norm_mlp"])
    u = matmul(h, w["w_up"], layer=layer, epilogue="relu2", out_dtype=BF16, tm=2048, single_buffer_a=True)
    x = matmul(u, w["w_down"], layer=layer, epilogue="residual", residual=x, tm=1024, tn=1024, tk=2048)
    return x


_PARAM_NAMES = ("norm_mix", "w_in", "fox_q_gain", "fox_k_gain", "fox_f_bias", "rwkv_mu", "rwkv_w0", "rwkv_w2",
                "rwkv_a0", "rwkv_a2", "rwkv_g2", "rwkv_k_k", "rwkv_k_a", "rwkv_r_k", "rwkv_gn_gain",
                "rwkv_gn_bias", "w_out", "norm_mem_q", "norm_mem_kv", "mem_w_q", "mem_w_k", "mem_w_v",
                "mem_q_gain", "mem_k_gain", "mem_w_o", "norm_mlp", "w_up", "w_down")
_STACKED_WEIGHTS = ("w_in", "w_out", "mem_w_q", "mem_w_k", "mem_w_v", "mem_w_o", "w_up", "w_down")


def kernel(x, mem, norm_mix, w_in, fox_q_gain, fox_k_gain, fox_f_bias, rwkv_mu, rwkv_w0, rwkv_w2, rwkv_a0, rwkv_a2, rwkv_g2, rwkv_k_k, rwkv_k_a, rwkv_r_k, rwkv_gn_gain, rwkv_gn_bias, w_out, norm_mem_q, norm_mem_kv, mem_w_q, mem_w_k, mem_w_v, mem_q_gain, mem_k_gain, mem_w_o, norm_mlp, w_up, w_down):
    params = dict(zip(_PARAM_NAMES, (norm_mix, w_in, fox_q_gain, fox_k_gain, fox_f_bias, rwkv_mu, rwkv_w0,
                                     rwkv_w2, rwkv_a0, rwkv_a2, rwkv_g2, rwkv_k_k, rwkv_k_a, rwkv_r_k,
                                     rwkv_gn_gain, rwkv_gn_bias, w_out, norm_mem_q, norm_mem_kv, mem_w_q,
                                     mem_w_k, mem_w_v, mem_q_gain, mem_k_gain, mem_w_o, norm_mlp, w_up, w_down)))
    b, s, d = x.shape
    assert d == D_MODEL, x.shape
    depth = w_in.shape[0]
    stacked = {name: params[name] for name in _STACKED_WEIGHTS}
    x2 = x.reshape(b * s, d)
    mem2 = mem.reshape(-1, d)
    for layer in range(depth):
        small = {name: value[layer] for name, value in params.items() if name not in _STACKED_WEIGHTS}
        x2 = _layer(x2, mem2, b, s, layer, small, stacked)
    return x2.reshape(b, s, d)
```

```python
import functools

import jax
import jax.numpy as jnp
from jax import lax
from jax.experimental import pallas as pl
from jax.experimental.pallas import tpu as pltpu

D_MODEL = 4096
FOX_WIDTH = 2048
FOX_HEAD_DIM = 128
FOX_HEADS = 16
RWKV_WIDTH = 2048
RWKV_HEAD_DIM = 64
RWKV_HEADS = 32
DECAY_LORA = 128
AAA_LORA = 128
GATE_LORA = 480
MEM_HEADS = 4
MEM_HEAD_DIM = 128
MEM_WIDTH = MEM_HEADS * MEM_HEAD_DIM
NORM_EPS = 1e-6
GN_EPS = 64e-5
FOX_QKVG = 4 * FOX_WIDTH
FOX_IN = FOX_QKVG + FOX_HEADS
RWKV_IN = 3 * RWKV_WIDTH + DECAY_LORA + AAA_LORA + GATE_LORA
R_OFF, WLO_OFF, K_OFF, V_OFF, ALO_OFF, GLO_OFF = 0, 2048, 2176, 4224, 6272, 6400

LANES = 128
SUBLANES = 8
VMEM_BYTES_V7X = 64 * 1024 * 1024
RWKV_READ = 6912
RWKV_PAD = 7168
GATE_PAD = RWKV_READ - GLO_OFF

LANE_GROUPS = LANES // RWKV_HEADS
PACK_ROWS = RWKV_WIDTH // LANES

F32 = jnp.float32
BF16 = jnp.bfloat16
HIGHEST = lax.Precision.HIGHEST
LOG2_E = 1.4426950408889634


def _vmem_limit(block_bytes, scratch_bytes=0, temp_bytes=0):
    need = 2 * block_bytes + scratch_bytes + temp_bytes + (4 << 20)
    return int(min(need, VMEM_BYTES_V7X - (6 << 20)))


def _nbytes(shape, dtype):
    n = 1
    for s in shape:
        n *= s
    return n * jnp.dtype(dtype).itemsize


def _rmsnorm_kernel(x_ref, g_ref, o_ref):
    x = x_ref[...]
    ms = jnp.mean(x * x, axis=-1, keepdims=True)
    o_ref[...] = (x * lax.rsqrt(ms + NORM_EPS) * g_ref[...]).astype(o_ref.dtype)


def rmsnorm_bf16(x, gain, tm=512):
    m, d = x.shape
    tm = min(tm, m)
    return pl.pallas_call(
        _rmsnorm_kernel,
        grid=(m // tm,),
        in_specs=[pl.BlockSpec((tm, d), lambda i: (i, 0)),
                  pl.BlockSpec((1, d), lambda i: (0, 0))],
        out_specs=pl.BlockSpec((tm, d), lambda i: (i, 0)),
        out_shape=jax.ShapeDtypeStruct((m, d), BF16),
        compiler_params=pltpu.CompilerParams(
            dimension_semantics=("parallel",),
            vmem_limit_bytes=_vmem_limit(_nbytes((tm, d), F32) + _nbytes((tm, d), BF16),
                                         temp_bytes=2 * _nbytes((tm, d), F32))),
        name="rmsnorm",
    )(x, gain.reshape(1, d))


def _matmul_kernel(*refs, epilogue, nk, w_transposed):
    if epilogue == "residual":
        a_ref, w_ref, r_ref, o_ref = refs
    else:
        a_ref, w_ref, o_ref = refs

    def product():
        w = w_ref[...].astype(BF16)
        if w_transposed:
            return lax.dot_general(a_ref[...], w, (((1,), (1,)), ((), ())), preferred_element_type=F32)
        return jnp.dot(a_ref[...], w, preferred_element_type=F32)

    if nk == 1:
        d = product()
        if epilogue == "relu2":
            r = jnp.maximum(d, 0.0)
            o_ref[...] = (r * r).astype(o_ref.dtype)
        elif epilogue == "residual":
            o_ref[...] = r_ref[...] + d
        else:
            o_ref[...] = d.astype(o_ref.dtype)
    else:
        k = pl.program_id(2)

        @pl.when(k == 0)
        def _():
            o_ref[...] = r_ref[...] + product() if epilogue == "residual" else product()

        @pl.when(k > 0)
        def _():
            o_ref[...] += product()


def matmul(a, w, *, layer=None, n=None, epilogue="none", residual=None, out_dtype=F32,
           tm=1024, tn=512, tk=None, single_buffer_a=False, w_transposed=False):
    m, kdim = a.shape
    n = w.shape[-2 if w_transposed else -1] if n is None else n
    tk = kdim if tk is None else tk
    tm, tn = min(tm, m), min(tn, n)
    assert m % tm == 0 and n % tn == 0 and kdim % tk == 0, (a.shape, w.shape, tm, tn, tk)
    nk = kdim // tk
    assert nk == 1 or (out_dtype == F32 and epilogue != "relu2")
    w_block = (tn, tk) if w_transposed else (tk, tn)
    w_index = (lambda k, j: (j, k)) if w_transposed else (lambda k, j: (k, j))
    if layer is None:
        w_spec = pl.BlockSpec(w_block, lambda i, j, k: w_index(k, j))
    else:
        w_spec = pl.BlockSpec((None,) + w_block, lambda i, j, k: (layer,) + w_index(k, j))
    a_mode = pl.Buffered(1) if single_buffer_a else None
    in_specs = [pl.BlockSpec((tm, tk), lambda i, j, k: (i, k), pipeline_mode=a_mode), w_spec]
    args = [a, w]
    a_bytes = _nbytes((tm, tk), BF16)
    blk = (a_bytes // 2 if single_buffer_a else a_bytes) + _nbytes((tk, tn), w.dtype) + _nbytes((tm, tn), out_dtype)
    if epilogue == "residual":
        in_specs.append(pl.BlockSpec((tm, tn), lambda i, j, k: (i, j)))
        args.append(residual)
        blk += _nbytes((tm, tn), F32)
    return pl.pallas_call(
        functools.partial(_matmul_kernel, epilogue=epilogue, nk=nk, w_transposed=w_transposed),
        grid=(m // tm, n // tn, nk),
        in_specs=in_specs,
        out_specs=pl.BlockSpec((tm, tn), lambda i, j, k: (i, j)),
        out_shape=jax.ShapeDtypeStruct((m, n), out_dtype),
        compiler_params=pltpu.CompilerParams(
            dimension_semantics=("parallel", "parallel", "arbitrary"),
            vmem_limit_bytes=_vmem_limit(blk, temp_bytes=_nbytes((tk, tn), BF16) + 2 * _nbytes((tm, tn), F32))),
        name="matmul_" + epilogue,
    )(*args)


def _out_proj_kernel(a1_ref, a2_ref, w1_ref, w2_ref, r_ref, o_ref):
    d = jnp.dot(a1_ref[...], w1_ref[...].astype(BF16), preferred_element_type=F32)
    d = d + jnp.dot(a2_ref[...], w2_ref[...].astype(BF16), preferred_element_type=F32)
    o_ref[...] = r_ref[...] + d


def out_proj(y_fox, y_rwkv, w_out, layer, w_rwkv_rows, x, tm=1024, tn=512):
    m, half = y_fox.shape
    n = x.shape[1]
    tm = min(tm, m)
    assert m % tm == 0 and n % tn == 0, (m, n, tm, tn)
    blk = 2 * _nbytes((tm, half), BF16) + 2 * _nbytes((half, tn), F32) + 2 * _nbytes((tm, tn), F32)
    return pl.pallas_call(
        _out_proj_kernel,
        grid=(m // tm, n // tn),
        in_specs=[pl.BlockSpec((tm, half), lambda i, j: (i, 0)),
                  pl.BlockSpec((tm, half), lambda i, j: (i, 0)),
                  pl.BlockSpec((None, half, tn), lambda i, j: (layer, 0, j)),
                  pl.BlockSpec((half, tn), lambda i, j: (0, j)),
                  pl.BlockSpec((tm, tn), lambda i, j: (i, j))],
        out_specs=pl.BlockSpec((tm, tn), lambda i, j: (i, j)),
        out_shape=jax.ShapeDtypeStruct((m, n), F32),
        compiler_params=pltpu.CompilerParams(
            dimension_semantics=("parallel", "parallel"),
            vmem_limit_bytes=_vmem_limit(blk, temp_bytes=2 * _nbytes((half, tn), BF16) + 2 * _nbytes((tm, tn), F32))),
        name="out_proj",
    )(y_fox, y_rwkv, w_out, w_rwkv_rows, x)


def _fox_prep_kernel(p_ref, h_ref, wf_ref, fb_ref, qg_ref, kg_ref, tri_ref,
                     q_o, k_o, c_o, carry_ref, *, tq):
    @pl.when(pl.program_id(1) == 0)
    def _():
        carry_ref[...] = jnp.zeros_like(carry_ref)

    scale = FOX_HEAD_DIM ** -0.5 * LOG2_E
    for h in range(FOX_HEADS):
        lo = h * FOX_HEAD_DIM
        q = p_ref[:, lo:lo + FOX_HEAD_DIM].astype(F32)
        ms = jnp.mean(q * q, axis=-1, keepdims=True)
        q_o[:, lo:lo + FOX_HEAD_DIM] = (q * lax.rsqrt(ms + NORM_EPS) * qg_ref[...] * scale).astype(BF16)
        k = p_ref[:, FOX_WIDTH + lo:FOX_WIDTH + lo + FOX_HEAD_DIM].astype(F32)
        ms = jnp.mean(k * k, axis=-1, keepdims=True)
        k_o[:, lo:lo + FOX_HEAD_DIM] = (k * lax.rsqrt(ms + NORM_EPS) * kg_ref[...]).astype(BF16)

    f_logit = lax.dot_general(h_ref[...], wf_ref[...].astype(BF16), (((1,), (1,)), ((), ())),
                              preferred_element_type=F32)
    log_f = jax.nn.log_sigmoid(f_logit + fb_ref[...])
    c = jnp.dot(tri_ref[...], log_f, precision=HIGHEST, preferred_element_type=F32) + carry_ref[0:1, :]
    c_o[...] = c * LOG2_E
    carry_ref[...] = jnp.broadcast_to(c[tq - 1:tq, :], carry_ref.shape)


def fox_prep(pf3, h3, w_in_t, layer, f_bias, q_gain, k_gain, tq=512):
    b, s, _ = pf3.shape
    tq = min(tq, s)
    tri = (lax.broadcasted_iota(jnp.int32, (tq, tq), 0) >= lax.broadcasted_iota(jnp.int32, (tq, tq), 1)).astype(F32)
    qk_w = 2 * FOX_WIDTH
    act = jax.ShapeDtypeStruct((b, s, FOX_WIDTH), BF16)
    fb = jnp.pad(f_bias, (0, LANES - FOX_HEADS)).reshape(1, LANES)
    blk = (_nbytes((tq, qk_w), BF16) + _nbytes((tq, D_MODEL), BF16) + _nbytes((LANES, D_MODEL), F32)
           + _nbytes((tq, tq), F32) + 2 * _nbytes((tq, FOX_WIDTH), BF16) + _nbytes((tq, LANES), F32))
    qn, kn, c = pl.pallas_call(
        functools.partial(_fox_prep_kernel, tq=tq),
        grid=(b, s // tq),
        in_specs=[pl.BlockSpec((None, tq, qk_w), lambda bi, i: (bi, i, 0)),
                  pl.BlockSpec((None, tq, D_MODEL), lambda bi, i: (bi, i, 0)),
                  pl.BlockSpec((None, LANES, D_MODEL), lambda bi, i: (layer, FOX_QKVG // LANES, 0)),
                  pl.BlockSpec((1, LANES), lambda bi, i: (0, 0)),
                  pl.BlockSpec((1, FOX_HEAD_DIM), lambda bi, i: (0, 0)),
                  pl.BlockSpec((1, FOX_HEAD_DIM), lambda bi, i: (0, 0)),
                  pl.BlockSpec((tq, tq), lambda bi, i: (0, 0))],
        out_specs=[pl.BlockSpec((None, tq, FOX_WIDTH), lambda bi, i: (bi, i, 0)),
                   pl.BlockSpec((None, tq, FOX_WIDTH), lambda bi, i: (bi, i, 0)),
                   pl.BlockSpec((None, tq, LANES), lambda bi, i: (bi, i, 0))],
        out_shape=[act, act, jax.ShapeDtypeStruct((b, s, LANES), F32)],
        scratch_shapes=[pltpu.VMEM((SUBLANES, LANES), F32)],
        compiler_params=pltpu.CompilerParams(
            dimension_semantics=("parallel", "arbitrary"),
            vmem_limit_bytes=_vmem_limit(blk, temp_bytes=_nbytes((tq, qk_w), F32))),
        name="fox_prep",
    )(pf3, h3, w_in_t, fb, q_gain.reshape(1, -1), k_gain.reshape(1, -1), tri)
    return qn, kn, c[:, :, :FOX_HEADS].transpose(0, 2, 1)


FOX_HEADS_PER_STEP = 1


def _fox_attn_kernel(q_ref, k_ref, v_ref, c_ref, gate_ref, o_ref, m_ref, l_ref, acc_ref, s_ref, p_ref, *, t):
    qi = pl.program_id(2)
    heads = range(FOX_HEADS_PER_STEP)
    m_ref[...] = jnp.full_like(m_ref, -jnp.inf)
    l_ref[...] = jnp.zeros_like(l_ref)
    acc_ref[...] = jnp.zeros_like(acc_ref)
    for hh in heads:
        p_ref[hh, 1] = jnp.zeros((t, t), BF16)

    def cols(hh):
        return slice(hh * FOX_HEAD_DIM, (hh + 1) * FOX_HEAD_DIM)

    def rows(ref, blk, hh):
        return ref[pl.ds(pl.multiple_of(blk * t, t), t), cols(hh)]

    c_base = [c_ref[hh, pl.ds(qi, 1), :][:, 0:1] for hh in heads]

    def scores(ki, hh):
        s = lax.dot_general(q_ref[:, cols(hh)], rows(k_ref, ki, hh), (((1,), (1,)), ((), ())),
                            preferred_element_type=F32)
        return s + (c_base[hh] - c_ref[hh, pl.ds(ki, 1), :])

    def weighted_values(slot, blk, hh):
        return jnp.dot(p_ref[hh, slot], rows(v_ref, blk, hh), preferred_element_type=F32)

    def step(ki, masked):
        for hh in heads:
            s = s_ref[hh, ki % 2]
            pv_prev = weighted_values((ki + 1) % 2, jnp.maximum(ki - 1, 0), hh)
            if not masked:
                s_ref[hh, (ki + 1) % 2] = scores(ki + 1, hh)
            else:
                row = lax.broadcasted_iota(jnp.int32, (t, t), 0)
                col = lax.broadcasted_iota(jnp.int32, (t, t), 1)
                s = jnp.where(col <= row, s, -jnp.inf)
            m_prev = m_ref[hh]
            m_new = jnp.maximum(m_prev, jnp.max(s, axis=1, keepdims=True))
            alpha = jnp.exp2(m_prev - m_new)
            p = jnp.exp2(s - jnp.tile(m_new, (1, t // LANES)))
            l_ref[hh] = alpha * l_ref[hh] + jnp.sum(p, axis=1, keepdims=True)
            acc_ref[hh] = alpha * (acc_ref[hh] + pv_prev)
            m_ref[hh] = m_new
            p_ref[hh, ki % 2] = p.astype(BF16)

    for hh in heads:
        s_ref[hh, 0] = scores(0, hh)

    def body(ki, carry):
        step(ki, False)
        return carry

    lax.fori_loop(0, qi, body, 0)
    step(qi, True)
    for hh in heads:
        o = (acc_ref[hh] + weighted_values(qi % 2, qi, hh)) / l_ref[hh]
        o_ref[:, cols(hh)] = (o * jax.nn.sigmoid(gate_ref[:, cols(hh)].astype(F32))).astype(o_ref.dtype)


def fox_attention(qn, kn, c, pf3, t=512):
    b, s, _ = qn.shape
    t = min(t, s)
    nt = s // t
    hp = FOX_HEADS_PER_STEP
    width = hp * FOX_HEAD_DIM
    c4 = c.reshape(b, FOX_HEADS, nt, t)
    value_blk0 = 2 * FOX_WIDTH // width
    gate_blk0 = 3 * FOX_WIDTH // width
    blk = 3 * _nbytes((t, width), BF16) + 2 * _nbytes((s, width), BF16) + hp * _nbytes((nt, t), F32)
    scratch = hp * (3 * _nbytes((t, LANES), F32) + 3 * _nbytes((t, t), F32))
    return pl.pallas_call(
        functools.partial(_fox_attn_kernel, t=t),
        grid=(b, FOX_HEADS // hp, nt),
        in_specs=[pl.BlockSpec((None, t, width), lambda bi, h, qi: (bi, qi, h)),
                  pl.BlockSpec((None, s, width), lambda bi, h, qi: (bi, 0, h)),
                  pl.BlockSpec((None, s, width), lambda bi, h, qi: (bi, 0, value_blk0 + h)),
                  pl.BlockSpec((None, hp, nt, t), lambda bi, h, qi: (bi, h, 0, 0)),
                  pl.BlockSpec((None, t, width), lambda bi, h, qi: (bi, qi, gate_blk0 + h))],
        out_specs=pl.BlockSpec((None, t, width), lambda bi, h, qi: (bi, qi, h)),
        out_shape=jax.ShapeDtypeStruct((b, s, FOX_WIDTH), BF16),
        scratch_shapes=[pltpu.VMEM((hp, t, LANES), F32), pltpu.VMEM((hp, t, LANES), F32),
                        pltpu.VMEM((hp, t, FOX_HEAD_DIM), F32),
                        pltpu.VMEM((hp, 2, t, t), F32), pltpu.VMEM((hp, 2, t, t), BF16)],
        compiler_params=pltpu.CompilerParams(
            dimension_semantics=("parallel", "parallel", "arbitrary"),
            vmem_limit_bytes=_vmem_limit(blk, scratch_bytes=scratch, temp_bytes=8 * _nbytes((t, t), F32))),
        name="fox_attention",
    )(qn, kn, pf3, c4, pf3)


def _to_column_layout(x, axis=-1):
    x = jnp.moveaxis(x, axis, -1)
    y = x.reshape(x.shape[:-1] + (RWKV_HEADS, RWKV_HEAD_DIM)).swapaxes(-1, -2).reshape(x.shape)
    return jnp.moveaxis(y, -1, axis)


def _to_value_layout(x, axis=-1):
    x = jnp.moveaxis(x, axis, -1)
    lead = x.ndim - 1
    y = x.reshape(x.shape[:-1] + (RWKV_HEADS, 2, LANE_GROUPS, SUBLANES))
    y = y.transpose(tuple(range(lead)) + (lead + 1, lead + 3, lead + 2, lead)).reshape(x.shape)
    return jnp.moveaxis(y, -1, axis)


def _group_allreduce(x, axis):
    x = x + pltpu.roll(x, 2 * RWKV_HEADS, axis=axis)
    return x + pltpu.roll(x, RWKV_HEADS, axis=axis)


def _head_sum(x):
    s = x[:, 0:LANES]
    for row in range(1, PACK_ROWS):
        s = s + x[:, row * LANES:(row + 1) * LANES]
    return _group_allreduce(s, 1)


def _rwkv_prep_kernel(p_ref, pprev_ref, mu_ref, w0_ref, w2_ref, a0_ref, a2_ref, g2_ref, kk_ref, ka_ref,
                      rk_ref, r_o, w_o, k_o, v_o, a_o, b_o, g_o, bonus_o, *, tiles_per_seq):
    first = (pl.program_id(0) % tiles_per_seq) == 0

    def shifted(lo, hi):
        p = p_ref[:, lo:hi]
        prev_row = jnp.where(first, 0.0, pprev_ref[SUBLANES - 1:SUBLANES, lo:hi])
        row = lax.broadcasted_iota(jnp.int32, p.shape, 0)
        prev = jnp.where(row == 0, prev_row, pltpu.roll(p, 1, axis=0))
        return p + (prev - p) * mu_ref[:, lo:hi]

    def over_rows(s):
        return jnp.tile(s, (1, PACK_ROWS))

    w_lo = shifted(WLO_OFF, WLO_OFF + DECAY_LORA)
    z = w0_ref[...] + jnp.dot(jnp.tanh(w_lo), w2_ref[...], precision=HIGHEST, preferred_element_type=F32)
    softplus_neg = jnp.maximum(-z, 0.0) + jnp.log1p(jnp.exp(-jnp.abs(z)))
    w_o[...] = jnp.exp(-jnp.exp(-softplus_neg - 0.5))

    a_lo = shifted(ALO_OFF, ALO_OFF + AAA_LORA)
    a_lr = jax.nn.sigmoid(a0_ref[...] + jnp.dot(a_lo, a2_ref[...], precision=HIGHEST,
                                                preferred_element_type=F32))

    g_lo = shifted(GLO_OFF, GLO_OFF + GATE_PAD)
    g_o[...] = jnp.dot(jax.nn.sigmoid(g_lo).astype(BF16), g2_ref[...], preferred_element_type=F32)

    k = shifted(K_OFF, K_OFF + RWKV_WIDTH)
    kk = k * kk_ref[...]
    kk = kk * over_rows(lax.rsqrt(jnp.maximum(_head_sum(kk * kk), 1e-24)))
    a_o[...] = -kk
    b_o[...] = kk * a_lr
    k = k * (1.0 + (a_lr - 1.0) * ka_ref[...])
    k_o[...] = k

    r = shifted(R_OFF, R_OFF + RWKV_WIDTH)
    r_o[...] = r
    v = shifted(V_OFF, V_OFF + RWKV_WIDTH)
    v_o[...] = v
    bonus_o[...] = over_rows(_head_sum(r * k * rk_ref[...])) * v


def rwkv_prep(pr, seq, mu, w0, w2, a0, a2, g2p, k_k, k_a, r_k, tq=128):
    t_total = pr.shape[0]
    width = RWKV_READ
    tq = min(tq, seq)
    row = lambda x: x.reshape(1, -1)
    full = lambda shape: pl.BlockSpec(shape, lambda i: (0, 0))
    out = jax.ShapeDtypeStruct((t_total, RWKV_WIDTH), F32)
    out_spec = pl.BlockSpec((tq, RWKV_WIDTH), lambda i: (i, 0))
    sub_per_tile = tq // SUBLANES
    blk = (_nbytes((tq, width), F32) + _nbytes((SUBLANES, width), F32) + 8 * _nbytes((tq, RWKV_WIDTH), F32)
           + _nbytes((DECAY_LORA + AAA_LORA, RWKV_WIDTH), F32) + _nbytes((GATE_PAD, RWKV_WIDTH), BF16))
    return pl.pallas_call(
        functools.partial(_rwkv_prep_kernel, tiles_per_seq=seq // tq),
        grid=(t_total // tq,),
        in_specs=[pl.BlockSpec((tq, width), lambda i: (i, 0)),
                  pl.BlockSpec((SUBLANES, width), lambda i: (jnp.maximum(i * sub_per_tile - 1, 0), 0)),
                  full((1, width)), full((1, RWKV_WIDTH)), full((DECAY_LORA, RWKV_WIDTH)),
                  full((1, RWKV_WIDTH)), full((AAA_LORA, RWKV_WIDTH)), full((GATE_PAD, RWKV_WIDTH)),
                  full((1, RWKV_WIDTH)), full((1, RWKV_WIDTH)), full((1, RWKV_WIDTH))],
        out_specs=[out_spec] * 8,
        out_shape=[out] * 8,
        compiler_params=pltpu.CompilerParams(
            dimension_semantics=("parallel",),
            vmem_limit_bytes=_vmem_limit(blk, temp_bytes=12 * _nbytes((tq, RWKV_WIDTH), F32))),
        name="rwkv_prep",
    )(pr, pr, row(mu), row(w0), w2, row(a0), a2, g2p, row(k_k), row(k_a), row(r_k))


V_TILES = PACK_ROWS // SUBLANES
ACCUMULATORS = 4


def _rwkv_scan_kernel(r_ref, w_ref, k_ref, a_ref, b_ref, rn_ref, wn_ref, kn_ref, an_ref, bn_ref,
                      v_ref, bonus_ref, g_ref, gain_ref, bias_ref, o_ref,
                      state_ref, rep_even_ref, rep_odd_ref, y_ref, *, tc, nb):
    lane_group = lax.broadcasted_iota(jnp.int32, (1, 1, LANES), 2) // RWKV_HEADS

    def replicate(x, lanes):
        rolled = [x] + [pltpu.roll(x, j * RWKV_HEADS, axis=x.ndim - 1) for j in range(1, LANE_GROUPS)]
        outs = []
        for q in range(LANE_GROUPS):
            out = rolled[(LANE_GROUPS - 1 - q) % LANE_GROUPS]
            for grp in range(LANE_GROUPS - 2, -1, -1):
                out = jnp.where(lanes == grp, rolled[(grp - q) % LANE_GROUPS], out)
            outs.append(out)
        return outs

    halves = [slice(half * SUBLANES, (half + 1) * SUBLANES) for half in range(PACK_ROWS // SUBLANES)]

    current = (r_ref, w_ref, k_ref, a_ref, b_ref)
    upcoming = (rn_ref, wn_ref, kn_ref, an_ref, bn_ref)

    @pl.when(pl.program_id(0) == 0)
    def _():
        state_ref[...] = jnp.zeros_like(state_ref)
        for i, ref in enumerate(current):
            for b in range(nb):
                for rows in halves:
                    for q, out in enumerate(replicate(ref[b, 0:tc, rows, :], lane_group)):
                        rep_even_ref[i, b, :, q, rows, :] = out

    R, W, K, A, B = range(5)

    def tile(g):
        return pl.ds(g * SUBLANES, SUBLANES)

    def total(parts):
        while len(parts) > 1:
            parts = [parts[i] + parts[i + 1] for i in range(0, len(parts), 2)]
        return parts[0]

    columns = [(q, row) for q in range(LANE_GROUPS) for row in range(PACK_ROWS)]

    def run_chunk(base, rep_ref, fill_ref, fill_src, fill_base):
        def bcast(i, b, t, q, row):
            return jnp.broadcast_to(rep_ref[i, b, t, q, row:row + 1, :], (SUBLANES, LANES))

        def step(t, carry):
            for i, ref in enumerate(fill_src):
                for b in range(nb):
                    for rows in halves:
                        for q, out in enumerate(replicate(ref[b, fill_base + t, rows, :], lane_group[0])):
                            fill_ref[i, b, t, q, rows, :] = out
            for b in range(nb):
                sa = [[None] * ACCUMULATORS for _ in range(V_TILES)]
                for idx, (q, row) in enumerate(columns):
                    ab = bcast(A, b, t, q, row)
                    for g in range(V_TILES):
                        term = state_ref[b, q, row, tile(g), :] * ab
                        slot = idx % ACCUMULATORS
                        sa[g][slot] = term if sa[g][slot] is None else sa[g][slot] + term
                sa = [total(parts) for parts in sa]
                vt = [v_ref[b, base + t, tile(g), :] for g in range(V_TILES)]
                y = [[None] * ACCUMULATORS for _ in range(V_TILES)]
                for idx, (q, row) in enumerate(columns):
                    wb, bb = bcast(W, b, t, q, row), bcast(B, b, t, q, row)
                    kb, rb = bcast(K, b, t, q, row), bcast(R, b, t, q, row)
                    for g in range(V_TILES):
                        s = state_ref[b, q, row, tile(g), :] * wb + sa[g] * bb + vt[g] * kb
                        state_ref[b, q, row, tile(g), :] = s
                        slot = idx % ACCUMULATORS
                        y[g][slot] = s * rb if y[g][slot] is None else y[g][slot] + s * rb
                for g in range(V_TILES):
                    y_ref[b, base + t, tile(g), :] = total(y[g])
            return carry

        lax.fori_loop(0, tc, step, 0)

    run_chunk(0, rep_even_ref, rep_odd_ref, current, tc)
    run_chunk(tc, rep_odd_ref, rep_even_ref, upcoming, 0)

    inv_n = 1.0 / RWKV_HEAD_DIM
    for b in range(nb):
        y = y_ref[b]
        mean = _group_allreduce(jnp.sum(y, axis=1, keepdims=True), 2) * inv_n
        yc = y - mean
        var = _group_allreduce(jnp.sum(yc * yc, axis=1, keepdims=True), 2) * inv_n
        yn = yc * lax.rsqrt(var + GN_EPS)
        out = (yn * gain_ref[...] + bias_ref[...] + bonus_ref[b]) * g_ref[b]
        o_ref[b] = out.astype(o_ref.dtype)


def rwkv_scan(r, w, k, v, a, b, bonus, g, gn_gain, gn_bias, batch, tc=32):
    t_total = r.shape[0]
    s = t_total // batch
    tc = min(tc, s // 2)
    n_steps = s // (2 * tc)
    assert s == n_steps * 2 * tc, (s, tc)
    packed = lambda x: x.reshape(batch, s, PACK_ROWS, LANES)
    spec = pl.BlockSpec((batch, 2 * tc, PACK_ROWS, LANES), lambda i: (0, i, 0, 0))
    next_spec = pl.BlockSpec((batch, tc, PACK_ROWS, LANES),
                             lambda i: (0, jnp.minimum(2 * i + 2, 2 * n_steps - 2), 0, 0))
    affine = pl.BlockSpec((PACK_ROWS, LANES), lambda i: (0, 0))
    chunk_bytes = _nbytes((batch, tc, PACK_ROWS, LANES), F32)
    blk = (2 * 8 + 5 + 1) * chunk_bytes
    state_shape = (batch, LANE_GROUPS, PACK_ROWS, PACK_ROWS, LANES)
    rep_shape = (5, batch, tc, LANE_GROUPS, PACK_ROWS, LANES)
    y_shape = (batch, 2 * tc, PACK_ROWS, LANES)
    scratch = _nbytes(state_shape, F32) + 2 * _nbytes(rep_shape, F32) + _nbytes(y_shape, F32)
    per_column = [packed(x) for x in (r, w, k, a, b)]
    out = pl.pallas_call(
        functools.partial(_rwkv_scan_kernel, tc=tc, nb=batch),
        grid=(n_steps,),
        in_specs=[spec] * 5 + [next_spec] * 5 + [spec] * 3 + [affine, affine],
        out_specs=spec,
        out_shape=jax.ShapeDtypeStruct((batch, s, PACK_ROWS, LANES), BF16),
        scratch_shapes=[pltpu.VMEM(state_shape, F32), pltpu.VMEM(rep_shape, F32), pltpu.VMEM(rep_shape, F32),
                        pltpu.VMEM(y_shape, F32)],
        compiler_params=pltpu.CompilerParams(
            dimension_semantics=("arbitrary",),
            vmem_limit_bytes=_vmem_limit(blk, scratch_bytes=scratch, temp_bytes=8 * _nbytes(y_shape, F32))),
        name="rwkv_scan",
    )(*per_column, *per_column, packed(v), packed(bonus), packed(g),
      gn_gain.reshape(PACK_ROWS, LANES), gn_bias.reshape(PACK_ROWS, LANES))
    return out.reshape(t_total, RWKV_WIDTH)


def _mem_attn_kernel(q_ref, k_ref, v_ref, qg_ref, kg_ref, o_ref):
    scale = MEM_HEAD_DIM ** -0.5
    for h in range(MEM_HEADS):
        cols = slice(h * MEM_HEAD_DIM, (h + 1) * MEM_HEAD_DIM)
        q = q_ref[:, cols]
        q = q * lax.rsqrt(jnp.mean(q * q, axis=-1, keepdims=True) + NORM_EPS) * qg_ref[...]
        k = k_ref[:, cols]
        k = k * lax.rsqrt(jnp.mean(k * k, axis=-1, keepdims=True) + NORM_EPS) * kg_ref[...]
        s = lax.dot_general(q.astype(BF16), k.astype(BF16), (((1,), (1,)), ((), ())),
                            preferred_element_type=F32) * scale
        p = jnp.exp(s - jnp.max(s, axis=-1, keepdims=True))
        p = p / jnp.sum(p, axis=-1, keepdims=True)
        o = jnp.dot(p.astype(BF16), v_ref[:, cols].astype(BF16), preferred_element_type=F32)
        o_ref[:, cols] = o.astype(o_ref.dtype)


def mem_attention(q3, k3, v3, q_gain, k_gain, tq=512):
    b, s, _ = q3.shape
    m = k3.shape[1]
    tq = min(tq, s)
    blk = _nbytes((tq, MEM_WIDTH), F32) + 2 * _nbytes((m, MEM_WIDTH), F32) + _nbytes((tq, MEM_WIDTH), BF16)
    return pl.pallas_call(
        _mem_attn_kernel,
        grid=(b, s // tq),
        in_specs=[pl.BlockSpec((None, tq, MEM_WIDTH), lambda bi, i: (bi, i, 0)),
                  pl.BlockSpec((None, m, MEM_WIDTH), lambda bi, i: (bi, 0, 0)),
                  pl.BlockSpec((None, m, MEM_WIDTH), lambda bi, i: (bi, 0, 0)),
                  pl.BlockSpec((1, MEM_HEAD_DIM), lambda bi, i: (0, 0)),
                  pl.BlockSpec((1, MEM_HEAD_DIM), lambda bi, i: (0, 0))],
        out_specs=pl.BlockSpec((None, tq, MEM_WIDTH), lambda bi, i: (bi, i, 0)),
        out_shape=jax.ShapeDtypeStruct((b, s, MEM_WIDTH), BF16),
        compiler_params=pltpu.CompilerParams(
            dimension_semantics=("parallel", "parallel"),
            vmem_limit_bytes=_vmem_limit(blk, temp_bytes=8 * _nbytes((tq, m), F32))),
        name="mem_attention",
    )(q3, k3, v3, q_gain.reshape(1, -1), k_gain.reshape(1, -1))


def _projection_tail_kernel(w_ref, o_ref, *, tn, valid):
    row = pl.program_id(0) * tn + lax.broadcasted_iota(jnp.int32, w_ref.shape, 0)
    o_ref[...] = jnp.where(row < valid, w_ref[...], 0.0).astype(o_ref.dtype)


def projection_tail(w_in_t, layer, tn=512):
    _, total, kdim = w_in_t.shape
    return pl.pallas_call(
        functools.partial(_projection_tail_kernel, tn=tn, valid=total - FOX_QKVG),
        grid=(RWKV_PAD // tn,),
        in_specs=[pl.BlockSpec((None, tn, kdim), lambda j: (layer, FOX_QKVG // tn + j, 0))],
        out_specs=pl.BlockSpec((tn, kdim), lambda j: (j, 0)),
        out_shape=jax.ShapeDtypeStruct((RWKV_PAD, kdim), BF16),
        compiler_params=pltpu.CompilerParams(
            dimension_semantics=("parallel",),
            vmem_limit_bytes=_vmem_limit(_nbytes((tn, kdim), F32) + _nbytes((tn, kdim), BF16),
                                         temp_bytes=_nbytes((tn, kdim), F32))),
        name="projection_tail",
    )(w_in_t)


def _rwkv_projection_weight(w_in_t, layer):
    tail = projection_tail(w_in_t, layer)

    def rows(lo, hi):
        return tail[FOX_HEADS + lo:FOX_HEADS + hi]

    parts = [_to_column_layout(rows(R_OFF, WLO_OFF), axis=0), rows(WLO_OFF, K_OFF),
             _to_column_layout(rows(K_OFF, V_OFF), axis=0), _to_value_layout(rows(V_OFF, ALO_OFF), axis=0),
             rows(ALO_OFF, RWKV_IN), jnp.zeros((RWKV_PAD - RWKV_IN, w_in_t.shape[2]), BF16)]
    return jnp.concatenate(parts, axis=0)


def _rwkv_shift_mix(mu):
    parts = [_to_column_layout(mu[R_OFF:WLO_OFF]), mu[WLO_OFF:K_OFF], _to_column_layout(mu[K_OFF:V_OFF]),
             _to_value_layout(mu[V_OFF:ALO_OFF]), mu[ALO_OFF:], jnp.zeros((RWKV_READ - RWKV_IN,), mu.dtype)]
    return jnp.concatenate(parts)


def _layer(x, mem2, b, s, layer, p, w):
    t_total = b * s
    h = rmsnorm_bf16(x, p["norm_mix"])
    w_in_t = jnp.swapaxes(w["w_in"], 1, 2)
    pf = matmul(h, w_in_t, layer=layer, n=FOX_QKVG, out_dtype=BF16, tm=2048, single_buffer_a=True,
                w_transposed=True)
    pr = matmul(h, _rwkv_projection_weight(w_in_t, layer), tm=2048, single_buffer_a=True, w_transposed=True)

    pf3 = pf.reshape(b, s, FOX_QKVG)
    qn, kn, c = fox_prep(pf3, h.reshape(b, s, D_MODEL), w_in_t, layer, p["fox_f_bias"],
                         p["fox_q_gain"], p["fox_k_gain"])
    y_fox = fox_attention(qn, kn, c, pf3).reshape(t_total, FOX_WIDTH)

    g2p = _to_value_layout(jnp.pad(p["rwkv_g2"], ((0, GATE_PAD - GATE_LORA), (0, 0)))).astype(BF16)
    r, wd, k, v, a, bb, g, bonus = rwkv_prep(
        pr, s, _rwkv_shift_mix(p["rwkv_mu"]), _to_column_layout(p["rwkv_w0"]), _to_column_layout(p["rwkv_w2"]),
        _to_column_layout(p["rwkv_a0"]), _to_column_layout(p["rwkv_a2"]), g2p, _to_column_layout(p["rwkv_k_k"]),
        _to_column_layout(p["rwkv_k_a"]), _to_column_layout(p["rwkv_r_k"].reshape(-1)))
    y_rwkv = rwkv_scan(r, wd, k, v, a, bb, bonus, g, _to_value_layout(p["rwkv_gn_gain"]),
                       _to_value_layout(p["rwkv_gn_bias"]), b)

    w_rwkv_rows = _to_value_layout(w["w_out"][layer, FOX_WIDTH:, :], axis=0)
    x = out_proj(y_fox, y_rwkv, w["w_out"], layer, w_rwkv_rows, x)

    h = rmsnorm_bf16(x, p["norm_mem_q"])
    m = rmsnorm_bf16(mem2, p["norm_mem_kv"])
    q = matmul(h, w["mem_w_q"], layer=layer)
    km = matmul(m, w["mem_w_k"], layer=layer)
    vm = matmul(m, w["mem_w_v"], layer=layer)
    n_mem = mem2.shape[0] // b
    o = mem_attention(q.reshape(b, s, MEM_WIDTH), km.reshape(b, n_mem, MEM_WIDTH),
                      vm.reshape(b, n_mem, MEM_WIDTH), p["mem_q_gain"], p["mem_k_gain"])
    x = matmul(o.reshape(t_total, MEM_WIDTH), w["mem_w_o"], layer=layer, epilogue="residual", residual=x,
               tn=1024)

    h = rmsnorm_bf16(x, p["norm_mlp"])
    u = matmul(h, w["w_up"], layer=layer, epilogue="relu2", out_dtype=BF16, tm=2048, single_buffer_a=True)
    x = matmul(u, w["w_down"], layer=layer, epilogue="residual", residual=x, tm=1024, tn=1024, tk=2048)
    return x


_PARAM_NAMES = ("norm_mix", "w_in", "fox_q_gain", "fox_k_gain", "fox_f_bias", "rwkv_mu", "rwkv_w0", "rwkv_w2",
                "rwkv_a0", "rwkv_a2", "rwkv_g2", "rwkv_k_k", "rwkv_k_a", "rwkv_r_k", "rwkv_gn_gain",
                "rwkv_gn_bias", "w_out", "norm_mem_q", "norm_mem_kv", "mem_w_q", "mem_w_k", "mem_w_v",
                "mem_q_gain", "mem_k_gain", "mem_w_o", "norm_mlp", "w_up", "w_down")
_STACKED_WEIGHTS = ("w_in", "w_out", "mem_w_q", "mem_w_k", "mem_w_v", "mem_w_o", "w_up", "w_down")


def kernel(x, mem, norm_mix, w_in, fox_q_gain, fox_k_gain, fox_f_bias, rwkv_mu, rwkv_w0, rwkv_w2, rwkv_a0, rwkv_a2, rwkv_g2, rwkv_k_k, rwkv_k_a, rwkv_r_k, rwkv_gn_gain, rwkv_gn_bias, w_out, norm_mem_q, norm_mem_kv, mem_w_q, mem_w_k, mem_w_v, mem_q_gain, mem_k_gain, mem_w_o, norm_mlp, w_up, w_down):
    params = dict(zip(_PARAM_NAMES, (norm_mix, w_in, fox_q_gain, fox_k_gain, fox_f_bias, rwkv_mu, rwkv_w0,
                                     rwkv_w2, rwkv_a0, rwkv_a2, rwkv_g2, rwkv_k_k, rwkv_k_a, rwkv_r_k,
                                     rwkv_gn_gain, rwkv_gn_bias, w_out, norm_mem_q, norm_mem_kv, mem_w_q,
                                     mem_w_k, mem_w_v, mem_q_gain, mem_k_gain, mem_w_o, norm_mlp, w_up, w_down)))
    b, s, d = x.shape
    assert d == D_MODEL, x.shape
    depth = w_in.shape[0]
    stacked = {name: params[name] for name in _STACKED_WEIGHTS}
    x2 = x.reshape(b * s, d)
    mem2 = mem.reshape(-1, d)
    for layer in range(depth):
        small = {name: value[layer] for name, value in params.items() if name not in _STACKED_WEIGHTS}
        x2 = _layer(x2, mem2, b, s, layer, small, stacked)
    return x2.reshape(b, s, d)
```
